```python
import jax, jax.numpy as jnp
from jax import lax
import numpy as np

D_MODEL = 2048
BATCH = 2
SEQ = 8192
DEPTH = 4

N_MIXERS = 2
FOX_HEADS = 16
FOX_HEAD_DIM = D_MODEL // FOX_HEADS
FOX_WIDTH = FOX_HEADS * FOX_HEAD_DIM
Q_BLOCK = 128
CONV_CHANNELS = D_MODEL
CONV_KERNEL = 31
RMS_EPS = 1e-6
LN_EPS = 1e-5
N_FOX = (DEPTH + 1) // 2
N_CONV = DEPTH // 2

kernel_name = 'hybrid_fox_conformer_conv_trunk'


def rmsnorm(x, g):
    xf = x.astype(jnp.float32)
    y = xf * lax.rsqrt(jnp.mean(xf * xf, axis=-1, keepdims=True) + RMS_EPS) * g.astype(jnp.float32)
    return y.astype(x.dtype)


def layernorm(x, g, b):
    xf = x.astype(jnp.float32)
    mu = jnp.mean(xf, axis=-1, keepdims=True)
    var = jnp.mean(jnp.square(xf - mu), axis=-1, keepdims=True)
    y = (xf - mu) * lax.rsqrt(var + LN_EPS) * g.astype(jnp.float32) + b.astype(jnp.float32)
    return y.astype(x.dtype)


def fox_mixer(h, w_in, b_f, w_out):
    B, S, _ = h.shape
    W, H, Dh = FOX_WIDTH, FOX_HEADS, FOX_HEAD_DIM
    proj = h @ w_in
    q, k, v, gate, f_logit = jnp.split(proj, [W, 2 * W, 3 * W, 4 * W], axis=-1)
    q = q.reshape(B, S, H, Dh).transpose(0, 2, 1, 3)
    kf = k.reshape(B, S, H, Dh).transpose(0, 2, 1, 3).astype(jnp.float32)
    v = v.reshape(B, S, H, Dh).transpose(0, 2, 1, 3)
    log_f = jax.nn.log_sigmoid((f_logit + b_f).astype(jnp.float32))
    c = jnp.cumsum(log_f, axis=1).transpose(0, 2, 1)
    nb = S // Q_BLOCK
    q_blocks = q.reshape(B, H, nb, Q_BLOCK, Dh).transpose(2, 0, 1, 3, 4)
    c_blocks = c.reshape(B, H, nb, Q_BLOCK).transpose(2, 0, 1, 3)
    starts = jnp.arange(nb, dtype=jnp.int32) * Q_BLOCK
    k_pos = jnp.arange(S, dtype=jnp.int32)
    scale = FOX_HEAD_DIM ** -0.5

    def attend(args):
        q_blk, c_blk, start = args
        s = jnp.einsum('bhqd,bhkd->bhqk', q_blk.astype(jnp.float32), kf) * scale
        s = s + c_blk[..., :, None] - c[:, :, None, :]
        q_pos = start + jnp.arange(Q_BLOCK, dtype=jnp.int32)
        s = jnp.where(q_pos[:, None] >= k_pos[None, :], s, -jnp.inf)
        p = jax.nn.softmax(s, axis=-1)
        return jnp.einsum('bhqk,bhkd->bhqd', p.astype(v.dtype), v)

    o = lax.map(attend, (q_blocks, c_blocks, starts))
    o = o.transpose(1, 0, 3, 2, 4).reshape(B, S, W)
    y = o * jax.nn.silu(gate)
    return y @ w_out


def conv_mixer(h, w_in, b_in, dw, dw_b, ln_g, ln_b, w_out):
    C = CONV_CHANNELS
    proj = h @ w_in + b_in
    a, b, gate = jnp.split(proj, [C, 2 * C], axis=-1)
    u = a * jax.nn.sigmoid(b)
    u = lax.conv_general_dilated(
        u, dw[:, None, :].astype(u.dtype), window_strides=(1,),
        padding=[(CONV_KERNEL - 1, 0)],
        dimension_numbers=('NWC', 'WIO', 'NWC'),
        feature_group_count=C) + dw_b
    u = jax.nn.silu(layernorm(u, ln_g, ln_b))
    y = u * jax.nn.silu(gate)
    return y @ w_out


def setup_inputs(seed: int = 0) -> dict:
    key = jax.random.key(seed)
    ks = jax.random.split(key, 16)
    D, W, H, C, K = D_MODEL, FOX_WIDTH, FOX_HEADS, CONV_CHANNELS, CONV_KERNEL
    nrm = jax.random.normal
    return {
        'x': nrm(ks[0], (BATCH, SEQ, D), jnp.float32),
        'norm_g': 1.0 + 0.02 * nrm(ks[1], (DEPTH, D), jnp.float32),
        'fox_w_in': nrm(ks[2], (N_FOX, D, 4 * W + H), jnp.float32) * D ** -0.5,
        'fox_b_f': 2.0 + 0.5 * nrm(ks[3], (N_FOX, H), jnp.float32),
        'fox_w_out': nrm(ks[4], (N_FOX, W, D), jnp.float32) * W ** -0.5,
        'conv_w_in': nrm(ks[5], (N_CONV, D, 3 * C), jnp.float32) * D ** -0.5,
        'conv_b_in': 0.02 * nrm(ks[6], (N_CONV, 3 * C), jnp.float32),
        'conv_dw': nrm(ks[7], (N_CONV, K, C), jnp.float32) * K ** -0.5,
        'conv_dw_b': 0.02 * nrm(ks[8], (N_CONV, C), jnp.float32),
        'conv_ln_g': 1.0 + 0.02 * nrm(ks[9], (N_CONV, C), jnp.float32),
        'conv_ln_b': 0.02 * nrm(ks[10], (N_CONV, C), jnp.float32),
        'conv_w_out': nrm(ks[11], (N_CONV, C, D), jnp.float32) * C ** -0.5,
        'final_norm_g': 1.0 + 0.02 * nrm(ks[12], (D,), jnp.float32),
    }


def reference(x, norm_g, fox_w_in, fox_b_f, fox_w_out, conv_w_in, conv_b_in, conv_dw,
              conv_dw_b, conv_ln_g, conv_ln_b, conv_w_out, final_norm_g):
    h = x
    for i in range(DEPTH):
        hn = rmsnorm(h, norm_g[i])
        j = i // N_MIXERS
        if i % N_MIXERS == 0:
            h = h + fox_mixer(hn, fox_w_in[j], fox_b_f[j], fox_w_out[j])
        else:
            h = h + conv_mixer(hn, conv_w_in[j], conv_b_in[j], conv_dw[j], conv_dw_b[j],
                               conv_ln_g[j], conv_ln_b[j], conv_w_out[j])
    return rmsnorm(h, final_norm_g)
```

```python
import functools
import math

import jax
import jax.numpy as jnp
from jax import lax
from jax.experimental import pallas as pl
from jax.experimental.pallas import tpu as pltpu

D_MODEL = 2048
DEPTH = 4
FOX_HEADS = 16
FOX_HEAD_DIM = 128
FOX_WIDTH = FOX_HEADS * FOX_HEAD_DIM
CONV_CHANNELS = D_MODEL
CONV_KERNEL = 31
RMS_EPS = 1e-6
LN_EPS = 1e-5

LANES = 128
LOG2E = 1.4426950408889634
VMEM_LIMIT = 56 * 1024 * 1024
HEAD_LANES = LANES // FOX_HEADS
MASK_VALUE = -1e30
CONV_HALO = 32

BF16 = jnp.bfloat16
F32 = jnp.float32


def _params(*sem):
    return pltpu.CompilerParams(dimension_semantics=sem, vmem_limit_bytes=VMEM_LIMIT)


def _split3(x):
    hi = x.astype(BF16).astype(F32)
    r = x - hi
    mid = r.astype(BF16).astype(F32)
    lo = (r - mid).astype(BF16).astype(F32)
    return hi, mid, lo


def _silu(x):
    return x / (1.0 + jnp.exp(-x))


def _rmsnorm_kernel(x_ref, g_ref, o_ref):
    x = x_ref[...]
    y = x * lax.rsqrt(jnp.mean(x * x, axis=-1, keepdims=True) + RMS_EPS) * g_ref[...]
    o_ref[...] = y.astype(o_ref.dtype)


def rmsnorm_bf16(x, g, *, tm=512):
    m, d = x.shape
    assert m % tm == 0
    return pl.pallas_call(
        _rmsnorm_kernel,
        grid=(m // tm,),
        in_specs=[pl.BlockSpec((tm, d), lambda i: (i, 0)), pl.BlockSpec((1, d), lambda i: (0, 0))],
        out_specs=pl.BlockSpec((tm, d), lambda i: (i, 0)),
        out_shape=jax.ShapeDtypeStruct((m, d), BF16),
        compiler_params=_params("parallel"),
        name="rmsnorm",
    )(x, g)


def _mm_nn_kernel(x_ref, w_ref, b_ref, o_ref):
    acc = jnp.dot(x_ref[...], w_ref[...], preferred_element_type=F32)
    o_ref[...] = (acc + b_ref[...]).astype(o_ref.dtype)


def matmul_nn(x, w, bias, *, tm=1024, tn=1024):
    m, k = x.shape
    n = w.shape[1]
    assert m % tm == 0 and n % tn == 0
    return pl.pallas_call(
        _mm_nn_kernel,
        grid=(m // tm, n // tn),
        in_specs=[
            pl.BlockSpec((tm, k), lambda i, j: (i, 0)),
            pl.BlockSpec((k, tn), lambda i, j: (0, j)),
            pl.BlockSpec((1, tn), lambda i, j: (0, j)),
        ],
        out_specs=pl.BlockSpec((tm, tn), lambda i, j: (i, j)),
        out_shape=jax.ShapeDtypeStruct((m, n), BF16),
        compiler_params=_params("parallel", "arbitrary"),
        name="proj_nn",
    )(x, w, bias)


def _mm_nt_kernel(wt_ref, x_ref, o_ref):
    acc = lax.dot_general(wt_ref[...], x_ref[0], (((1,), (1,)), ((), ())), preferred_element_type=F32)
    o_ref[0] = acc.astype(o_ref.dtype)


def matmul_nt(wt, x, *, tm=1024, tn=1024):
    n, k = wt.shape
    b, s, _ = x.shape
    assert s % tm == 0 and n % tn == 0
    return pl.pallas_call(
        _mm_nt_kernel,
        grid=(b, s // tm, n // tn),
        in_specs=[
            pl.BlockSpec((tn, k), lambda bi, i, j: (j, 0)),
            pl.BlockSpec((1, tm, k), lambda bi, i, j: (bi, i, 0)),
        ],
        out_specs=pl.BlockSpec((1, tn, tm), lambda bi, i, j: (bi, j, i)),
        out_shape=jax.ShapeDtypeStruct((b, n, s), BF16),
        compiler_params=_params("parallel", "parallel", "arbitrary"),
        name="proj_nt",
    )(wt, x)


def _fgate_kernel(x_ref, wf_ref, bf_ref, o_ref, carry_ref):
    t = x_ref.shape[1]

    @pl.when(pl.program_id(1) == 0)
    def _():
        carry_ref[...] = jnp.zeros_like(carry_ref)

    z = jnp.dot(x_ref[0], wf_ref[...], preferred_element_type=F32) + bf_ref[...]
    log_f = jnp.minimum(z, 0.0) - jnp.log(1.0 + jnp.exp(-jnp.abs(z)))
    row = lax.broadcasted_iota(jnp.int32, (t, t), 0)
    col = lax.broadcasted_iota(jnp.int32, (t, t), 1)
    tri = (col <= row).astype(BF16)
    pieces = jnp.concatenate(_split3(log_f), axis=1).astype(BF16)
    local = jnp.dot(tri, pieces, preferred_element_type=F32)
    c = carry_ref[...] + local[:, :LANES] + local[:, LANES:2 * LANES] + local[:, 2 * LANES:]
    carry_ref[...] = c[t - 1:t, :]
    hi, mid, lo = _split3(c * (-LOG2E))
    piece = lax.broadcasted_iota(jnp.int32, (t, LANES), 1) % HEAD_LANES
    packed = jnp.where(piece == 0, hi, jnp.where(piece == 1, mid, jnp.where(piece == 2, lo, 0.0)))
    o_ref[0] = packed.astype(o_ref.dtype)


def forget_gate(hn, wf, bf, *, t=512):
    b, s, k = hn.shape
    assert s % t == 0
    return pl.pallas_call(
        _fgate_kernel,
        grid=(b, s // t),
        in_specs=[
            pl.BlockSpec((1, t, k), lambda bi, i: (bi, i, 0)),
            pl.BlockSpec((k, LANES), lambda bi, i: (0, 0)),
            pl.BlockSpec((1, LANES), lambda bi, i: (0, 0)),
        ],
        out_specs=pl.BlockSpec((1, t, LANES), lambda bi, i: (bi, i, 0)),
        out_shape=jax.ShapeDtypeStruct((b, s, LANES), BF16),
        scratch_shapes=[pltpu.VMEM((1, LANES), F32)],
        compiler_params=_params("parallel", "arbitrary"),
        name="forget_gate",
    )(hn, wf, bf)


def _attn_kernel(qt_ref, k_ref, c_ref, vt_ref, g_ref, o_ref, kaug_ref, acc_ref, m_ref, l_ref, *, tk):
    h = pl.program_id(1)
    iq = pl.program_id(2)
    d, tq = qt_ref.shape[1], qt_ref.shape[2]

    @pl.when(iq == 0)
    def _():
        kaug_ref[:, :d] = k_ref[0]
        kaug_ref[:, d:] = c_ref[0]

    row = lax.broadcasted_iota(jnp.int32, (LANES, tq), 0)
    ones = ((row >= h * HEAD_LANES) & (row < h * HEAD_LANES + 3)).astype(BF16)
    qaug = jnp.concatenate([qt_ref[0], ones], axis=0)

    def scores(start):
        return jnp.dot(kaug_ref[pl.ds(start, tk), :], qaug, preferred_element_type=F32)

    def pv(start, p):
        return jnp.dot(vt_ref[0, :, pl.ds(start, tk)], p.astype(BF16), preferred_element_type=F32)

    q_pos = lax.broadcasted_iota(jnp.int32, (tk, tq), 1)
    k_pos = lax.broadcasted_iota(jnp.int32, (tk, tq), 0)
    for jd in range(tq // tk):
        start = pl.multiple_of(iq * tq + jd * tk, tk)
        s = jnp.where(k_pos + jd * tk <= q_pos, scores(start), MASK_VALUE)
        if jd == 0:
            m_new = jnp.max(s, axis=0, keepdims=True)
            p = jnp.exp2(s - m_new)
            l_ref[...] = jnp.sum(p, axis=0, keepdims=True)
            acc_ref[...] = pv(start, p)
        else:
            m_old = m_ref[...]
            m_new = jnp.maximum(m_old, jnp.max(s, axis=0, keepdims=True))
            alpha = jnp.exp2(m_old - m_new)
            p = jnp.exp2(s - m_new)
            l_ref[...] = alpha * l_ref[...] + jnp.sum(p, axis=0, keepdims=True)
            acc_ref[...] = alpha * acc_ref[...] + pv(start, p)
        m_ref[...] = m_new

    def body(j, carry):
        start = pl.multiple_of(j * tk, tk)
        s = scores(start)
        m_old = m_ref[...]
        m_new = jnp.maximum(m_old, jnp.max(s, axis=0, keepdims=True))
        alpha = jnp.exp2(m_old - m_new)
        p = jnp.exp2(s - m_new)
        l_ref[...] = alpha * l_ref[...] + jnp.sum(p, axis=0, keepdims=True)
        acc_ref[...] = alpha * acc_ref[...] + pv(start, p)
        m_ref[...] = m_new
        return carry

    lax.fori_loop(0, iq * (tq // tk), body, 0)

    o = (acc_ref[...] / l_ref[...]).T
    gate = g_ref[0].astype(F32)
    o_ref[0] = (o * _silu(gate)).astype(o_ref.dtype)


def fox_attention(qvt, kg, caug, *, tq=512, tk=512):
    b, _, s = qvt.shape
    hh, d = FOX_HEADS, FOX_HEAD_DIM
    assert s % tq == 0 and tq % tk == 0
    return pl.pallas_call(
        functools.partial(_attn_kernel, tk=tk),
        grid=(b, hh, s // tq),
        in_specs=[
            pl.BlockSpec((1, d, tq), lambda bi, h, i: (bi, h, i)),
            pl.BlockSpec((1, s, d), lambda bi, h, i: (bi, 0, h)),
            pl.BlockSpec((1, s, LANES), lambda bi, h, i: (bi, 0, 0)),
            pl.BlockSpec((1, d, s), lambda bi, h, i: (bi, hh + h, 0)),
            pl.BlockSpec((1, tq, d), lambda bi, h, i: (bi, i, hh + h)),
        ],
        out_specs=pl.BlockSpec((1, tq, d), lambda bi, h, i: (bi, i, h)),
        out_shape=jax.ShapeDtypeStruct((b, s, hh * d), BF16),
        scratch_shapes=[
            pltpu.VMEM((s, d + LANES), BF16),
            pltpu.VMEM((d, tq), F32),
            pltpu.VMEM((1, tq), F32),
            pltpu.VMEM((1, tq), F32),
        ],
        compiler_params=_params("parallel", "parallel", "arbitrary"),
        name="fox_attention",
    )(qvt, kg, caug, qvt, kg)


def _conv_kernel(a_ref, b_ref, g_ref, ah_ref, bh_ref, dw_ref, dwb_ref, lng_ref, lnb_ref, o_ref,
                 u_ref, y_ref, *, rows):
    ts, c = o_ref.shape[1], o_ref.shape[2]
    first = pl.program_id(1) == 0

    def glu(a, b):
        return a.astype(F32) / (1.0 + jnp.exp(-b.astype(F32)))

    u_ref[:CONV_HALO, :] = jnp.where(first, 0.0, glu(ah_ref[0], bh_ref[0]))
    u_ref[CONV_HALO:, :] = glu(a_ref[0], b_ref[0])

    shift = CONV_HALO - (CONV_KERNEL - 1)

    def chunk(ci, carry):
        lanes = pl.ds(pl.multiple_of(ci * LANES, LANES), LANES)
        taps = dw_ref[:, lanes]
        for r0 in range(0, ts, rows):
            acc = jnp.zeros((rows, LANES), F32)
            for kk in range(CONV_KERNEL):
                acc = acc + taps[kk:kk + 1, :] * u_ref[pl.ds(r0 + shift + kk, rows), lanes]
            y_ref[pl.ds(r0, rows), lanes] = acc
        return carry

    lax.fori_loop(0, c // LANES, chunk, 0)

    y = y_ref[...] + dwb_ref[...]
    mu = jnp.mean(y, axis=-1, keepdims=True)
    yc = y - mu
    var = jnp.mean(yc * yc, axis=-1, keepdims=True)
    z = yc * lax.rsqrt(var + LN_EPS) * lng_ref[...] + lnb_ref[...]
    o_ref[0] = (_silu(z) * _silu(g_ref[0].astype(F32))).astype(o_ref.dtype)


def conv_module(abg, dw, dw_b, ln_g, ln_b, *, ts=256, rows=64):
    b, s, c3 = abg.shape
    c = c3 // 3
    per = ts // CONV_HALO
    assert s % ts == 0 and ts % CONV_HALO == 0 and ts % rows == 0 and c % LANES == 0
    tile = lambda col: pl.BlockSpec((1, ts, c), lambda bi, i: (bi, i, col))
    halo = lambda col: pl.BlockSpec((1, CONV_HALO, c), lambda bi, i: (bi, jnp.maximum(i * per - 1, 0), col))
    vec = pl.BlockSpec((1, c), lambda bi, i: (0, 0))
    return pl.pallas_call(
        functools.partial(_conv_kernel, rows=rows),
        grid=(b, s // ts),
        in_specs=[tile(0), tile(1), tile(2), halo(0), halo(1),
                  pl.BlockSpec((CONV_KERNEL + 1, c), lambda bi, i: (0, 0)), vec, vec, vec],
        out_specs=pl.BlockSpec((1, ts, c), lambda bi, i: (bi, i, 0)),
        out_shape=jax.ShapeDtypeStruct((b, s, c), BF16),
        scratch_shapes=[pltpu.VMEM((ts + CONV_HALO, c), F32), pltpu.VMEM((ts, c), F32)],
        compiler_params=_params("parallel", "arbitrary"),
        name="conv_module",
    )(abg, abg, abg, abg, abg, dw, dw_b, ln_g, ln_b)


def _out_kernel(y_ref, w_ref, h_ref, g_ref, *out_refs, last):
    h_new = h_ref[...] + jnp.dot(y_ref[...], w_ref[...], preferred_element_type=F32)
    normed = h_new * lax.rsqrt(jnp.mean(h_new * h_new, axis=-1, keepdims=True) + RMS_EPS) * g_ref[...]
    if last:
        out_refs[0][...] = normed
    else:
        out_refs[0][...] = h_new
        out_refs[1][...] = normed.astype(BF16)


def out_projection(y, w, h, g, *, last, tm=256):
    m, k = y.shape
    d = w.shape[1]
    assert m % tm == 0
    row = lambda width: pl.BlockSpec((tm, width), lambda i: (i, 0))
    if last:
        out_specs, out_shape = row(d), jax.ShapeDtypeStruct((m, d), F32)
    else:
        out_specs = [row(d), row(d)]
        out_shape = [jax.ShapeDtypeStruct((m, d), F32), jax.ShapeDtypeStruct((m, d), BF16)]
    return pl.pallas_call(
        functools.partial(_out_kernel, last=last),
        grid=(m // tm,),
        in_specs=[row(k), pl.BlockSpec((k, d), lambda i: (0, 0)), row(d), pl.BlockSpec((1, d), lambda i: (0, 0))],
        out_specs=out_specs,
        out_shape=out_shape,
        compiler_params=_params("parallel"),
        name="out_proj",
    )(y, w, h, g)


def _fox_weights(w_in, b_f):
    w, hh = FOX_WIDTH, FOX_HEADS
    wq = w_in[:, :w] * (FOX_HEAD_DIM ** -0.5 * LOG2E)
    w_qvt = jnp.concatenate([wq, w_in[:, 2 * w:3 * w]], axis=1).T.astype(BF16)
    w_kg = jnp.concatenate([w_in[:, w:2 * w], w_in[:, 3 * w:4 * w]], axis=1).astype(BF16)
    used = (jnp.arange(LANES) % HEAD_LANES) < 3
    head = jnp.arange(LANES) // HEAD_LANES
    wf = jnp.where(used[None, :], w_in[:, 4 * w:][:, head], 0.0).astype(BF16)
    bf = jnp.where(used, b_f[head], 0.0)[None, :].astype(F32)
    return w_qvt, w_kg, wf, bf


def kernel(x, norm_g, fox_w_in, fox_b_f, fox_w_out, conv_w_in, conv_b_in, conv_dw, conv_dw_b, conv_ln_g,
           conv_ln_b, conv_w_out, final_norm_g):
    b, s, d = x.shape
    m = b * s
    h = x.reshape(m, d)
    hn = rmsnorm_bf16(h, norm_g[0][None, :])
    for i in range(DEPTH):
        j = i // 2
        if i % 2 == 0:
            w_qvt, w_kg, wf, bf = _fox_weights(fox_w_in[j], fox_b_f[j])
            hn3 = hn.reshape(b, s, d)
            kg = matmul_nn(hn, w_kg, jnp.zeros((1, w_kg.shape[1]), F32)).reshape(b, s, -1)
            qvt = matmul_nt(w_qvt, hn3)
            caug = forget_gate(hn3, wf, bf)
            y = fox_attention(qvt, kg, caug).reshape(m, FOX_WIDTH)
            w_out = fox_w_out[j].astype(BF16)
        else:
            abg = matmul_nn(hn, conv_w_in[j].astype(BF16), conv_b_in[j][None, :]).reshape(b, s, -1)
            dw = jnp.concatenate([conv_dw[j], jnp.zeros((1, CONV_CHANNELS), F32)], axis=0)
            y = conv_module(abg, dw, conv_dw_b[j][None, :], conv_ln_g[j][None, :],
                            conv_ln_b[j][None, :]).reshape(m, CONV_CHANNELS)
            w_out = conv_w_out[j].astype(BF16)
        if i + 1 < DEPTH:
            h, hn = out_projection(y, w_out, h, norm_g[i + 1][None, :], last=False)
        else:
            out = out_projection(y, w_out, h, final_norm_g[None, :], last=True)
    return out.reshape(b, s, d)
```

```python
import functools
import math

import jax
import jax.numpy as jnp
from jax import lax
from jax.experimental import pallas as pl
from jax.experimental.pallas import tpu as pltpu

D_MODEL = 2048
DEPTH = 4
FOX_HEADS = 16
FOX_HEAD_DIM = 128
FOX_WIDTH = FOX_HEADS * FOX_HEAD_DIM
CONV_CHANNELS = D_MODEL
CONV_KERNEL = 31
RMS_EPS = 1e-6
LN_EPS = 1e-5

LANES = 128
SUBLANES = 8
LOG2E = 1.4426950408889634
VMEM_LIMIT = 56 * 1024 * 1024
HEAD_LANES = LANES // FOX_HEADS
MASK_VALUE = -1e30
CONV_HALO = 32
STRIP = 16

BF16 = jnp.bfloat16
F32 = jnp.float32


def _params(*sem):
    return pltpu.CompilerParams(dimension_semantics=sem, vmem_limit_bytes=VMEM_LIMIT)


def _split3(x):
    hi = x.astype(BF16).astype(F32)
    r = x - hi
    mid = r.astype(BF16).astype(F32)
    lo = (r - mid).astype(BF16).astype(F32)
    return hi, mid, lo


def _sigmoid(x):
    return 0.5 * jnp.tanh(0.5 * x) + 0.5


def _silu(x):
    return x * _sigmoid(x)


def _rmsnorm_kernel(x_ref, g_ref, o_ref):
    x = x_ref[...]
    y = x * lax.rsqrt(jnp.mean(x * x, axis=-1, keepdims=True) + RMS_EPS) * g_ref[...]
    o_ref[...] = y.astype(o_ref.dtype)


def rmsnorm_bf16(x, g, *, tm=512):
    m, d = x.shape
    assert m % tm == 0
    return pl.pallas_call(
        _rmsnorm_kernel,
        grid=(m // tm,),
        in_specs=[pl.BlockSpec((tm, d), lambda i: (i, 0)), pl.BlockSpec((1, d), lambda i: (0, 0))],
        out_specs=pl.BlockSpec((tm, d), lambda i: (i, 0)),
        out_shape=jax.ShapeDtypeStruct((m, d), BF16),
        compiler_params=_params("parallel"),
        name="rmsnorm",
    )(x, g)


def _mm_nn_kernel(x_ref, w_ref, b_ref, o_ref):
    acc = jnp.dot(x_ref[...], w_ref[...], preferred_element_type=F32)
    o_ref[...] = (acc + b_ref[...]).astype(o_ref.dtype)


def matmul_nn(x, w, bias, *, tm=1024, tn=1024):
    m, k = x.shape
    n = w.shape[1]
    assert m % tm == 0 and n % tn == 0
    return pl.pallas_call(
        _mm_nn_kernel,
        grid=(m // tm, n // tn),
        in_specs=[
            pl.BlockSpec((tm, k), lambda i, j: (i, 0)),
            pl.BlockSpec((k, tn), lambda i, j: (0, j)),
            pl.BlockSpec((1, tn), lambda i, j: (0, j)),
        ],
        out_specs=pl.BlockSpec((tm, tn), lambda i, j: (i, j)),
        out_shape=jax.ShapeDtypeStruct((m, n), BF16),
        compiler_params=_params("parallel", "arbitrary"),
        name="proj_nn",
    )(x, w, bias)


def _mm_nt_kernel(wt_ref, x_ref, o_ref):
    acc = lax.dot_general(wt_ref[...], x_ref[0], (((1,), (1,)), ((), ())), preferred_element_type=F32)
    o_ref[0] = acc.astype(o_ref.dtype)


def matmul_nt(wt, x, *, tm=1024, tn=1024):
    n, k = wt.shape
    b, s, _ = x.shape
    assert s % tm == 0 and n % tn == 0
    return pl.pallas_call(
        _mm_nt_kernel,
        grid=(b, s // tm, n // tn),
        in_specs=[
            pl.BlockSpec((tn, k), lambda bi, i, j: (j, 0)),
            pl.BlockSpec((1, tm, k), lambda bi, i, j: (bi, i, 0)),
        ],
        out_specs=pl.BlockSpec((1, tn, tm), lambda bi, i, j: (bi, j, i)),
        out_shape=jax.ShapeDtypeStruct((b, n, s), BF16),
        compiler_params=_params("parallel", "parallel", "arbitrary"),
        name="proj_nt",
    )(wt, x)


def _fgate_kernel(x_ref, wf_ref, bf_ref, o_ref, carry_ref):
    t = x_ref.shape[1]

    @pl.when(pl.program_id(1) == 0)
    def _():
        carry_ref[...] = jnp.zeros_like(carry_ref)

    z = jnp.dot(x_ref[0], wf_ref[...], preferred_element_type=F32) + bf_ref[...]
    log_f = jnp.minimum(z, 0.0) - jnp.log(1.0 + jnp.exp(-jnp.abs(z)))
    row = lax.broadcasted_iota(jnp.int32, (t, t), 0)
    col = lax.broadcasted_iota(jnp.int32, (t, t), 1)
    tri = (col <= row).astype(BF16)
    pieces = jnp.concatenate(_split3(log_f), axis=1).astype(BF16)
    local = jnp.dot(tri, pieces, preferred_element_type=F32)
    c = carry_ref[...] + local[:, :LANES] + local[:, LANES:2 * LANES] + local[:, 2 * LANES:]
    carry_ref[...] = c[t - 1:t, :]
    hi, mid, lo = _split3(c * (-LOG2E))
    piece = lax.broadcasted_iota(jnp.int32, (t, LANES), 1) % HEAD_LANES
    packed = jnp.where(piece == 0, hi, jnp.where(piece == 1, mid, jnp.where(piece == 2, lo, 0.0)))
    o_ref[0] = packed.astype(o_ref.dtype)


def forget_gate(hn, wf, bf, *, t=512):
    b, s, k = hn.shape
    assert s % t == 0
    return pl.pallas_call(
        _fgate_kernel,
        grid=(b, s // t),
        in_specs=[
            pl.BlockSpec((1, t, k), lambda bi, i: (bi, i, 0)),
            pl.BlockSpec((k, LANES), lambda bi, i: (0, 0)),
            pl.BlockSpec((1, LANES), lambda bi, i: (0, 0)),
        ],
        out_specs=pl.BlockSpec((1, t, LANES), lambda bi, i: (bi, i, 0)),
        out_shape=jax.ShapeDtypeStruct((b, s, LANES), BF16),
        scratch_shapes=[pltpu.VMEM((1, LANES), F32)],
        compiler_params=_params("parallel", "arbitrary"),
        name="forget_gate",
    )(hn, wf, bf)


def _attn_kernel(qt_ref, k_ref, c_ref, vt_ref, g_ref, o_ref,
                 kaug_ref, qaug_ref, s_ref, p_ref, alpha_ref, acc_ref, m_ref, l_ref, *, tk):
    h = pl.program_id(1)
    iq = pl.program_id(2)
    d, tq = qt_ref.shape[1], qt_ref.shape[2]
    assert tq == 2 * tk

    @pl.when(iq == 0)
    def _():
        kaug_ref[:, :d] = k_ref[0]
        kaug_ref[:, d:] = c_ref[0]

    row = lax.broadcasted_iota(jnp.int32, (LANES, tq), 0)
    qaug_ref[:d, :] = qt_ref[0]
    qaug_ref[d:, :] = ((row >= h * HEAD_LANES) & (row < h * HEAD_LANES + 3)).astype(BF16)
    acc_ref[...] = jnp.zeros_like(acc_ref)
    l_ref[...] = jnp.zeros_like(l_ref)
    m_ref[...] = jnp.full_like(m_ref, MASK_VALUE)

    def stage_a(slot, start, diag=None):
        s = jnp.dot(kaug_ref[pl.ds(start, tk), :], qaug_ref[...], preferred_element_type=F32)
        if diag is not None:
            q_pos = lax.broadcasted_iota(jnp.int32, (tk, tq), 1)
            k_pos = lax.broadcasted_iota(jnp.int32, (tk, tq), 0)
            s = jnp.where(k_pos + diag * tk <= q_pos, s, MASK_VALUE)
        s_ref[slot] = s

    def stage_b(slot):
        s = s_ref[slot]
        m_old = m_ref[...]
        m_new = jnp.maximum(m_old, jnp.max(s, axis=0, keepdims=True))
        alpha = jnp.exp2(m_old - m_new)
        p = jnp.exp2(s - m_new)
        l_ref[...] = alpha * l_ref[...] + jnp.sum(p, axis=0, keepdims=True)
        m_ref[...] = m_new
        alpha_ref[slot] = alpha
        p_ref[slot] = p.astype(BF16)

    def stage_c(slot, start):
        pv = jnp.dot(vt_ref[0, :, pl.ds(start, tk)], p_ref[slot], preferred_element_type=F32)
        acc_ref[...] = alpha_ref[slot] * acc_ref[...] + pv

    base = pl.multiple_of(iq * tq, tq)
    stage_a(0, base, diag=0)
    stage_a(1, base + tk, diag=1)
    stage_b(0)

    def body(u, starts):
        c0, c1 = starts
        a0 = pl.multiple_of(u * tq, tq)
        a1 = a0 + tk
        stage_a(0, a0)
        stage_b(1)
        stage_c(0, pl.multiple_of(c0, tk))
        stage_a(1, a1)
        stage_b(0)
        stage_c(1, pl.multiple_of(c1, tk))
        return a0, a1

    c0, c1 = lax.fori_loop(0, iq, body, (base, base + tk))
    stage_b(1)
    stage_c(0, pl.multiple_of(c0, tk))
    stage_c(1, pl.multiple_of(c1, tk))

    o = (acc_ref[...] / l_ref[...]).T
    gate = g_ref[0].astype(F32)
    o_ref[0] = (o * _silu(gate)).astype(o_ref.dtype)


def fox_attention(qvt, kg, caug, *, tq=1024, tk=512):
    b, _, s = qvt.shape
    hh, d = FOX_HEADS, FOX_HEAD_DIM
    assert s % tq == 0 and tq == 2 * tk
    return pl.pallas_call(
        functools.partial(_attn_kernel, tk=tk),
        grid=(b, hh, s // tq),
        in_specs=[
            pl.BlockSpec((1, d, tq), lambda bi, h, i: (bi, h, i)),
            pl.BlockSpec((1, s, d), lambda bi, h, i: (bi, 0, h)),
            pl.BlockSpec((1, s, LANES), lambda bi, h, i: (bi, 0, 0)),
            pl.BlockSpec((1, d, s), lambda bi, h, i: (bi, hh + h, 0)),
            pl.BlockSpec((1, tq, d), lambda bi, h, i: (bi, i, hh + h)),
        ],
        out_specs=pl.BlockSpec((1, tq, d), lambda bi, h, i: (bi, i, h)),
        out_shape=jax.ShapeDtypeStruct((b, s, hh * d), BF16),
        scratch_shapes=[
            pltpu.VMEM((s, d + LANES), BF16),
            pltpu.VMEM((d + LANES, tq), BF16),
            pltpu.VMEM((2, tk, tq), F32),
            pltpu.VMEM((2, tk, tq), BF16),
            pltpu.VMEM((2, 1, tq), F32),
            pltpu.VMEM((d, tq), F32),
            pltpu.VMEM((1, tq), F32),
            pltpu.VMEM((1, tq), F32),
        ],
        compiler_params=_params("parallel", "parallel", "arbitrary"),
        name="fox_attention",
    )(qvt, kg, caug, qvt, kg)


def _conv_kernel(a_ref, b_ref, g_ref, ah_ref, bh_ref, dw_ref, dwb_ref, lng_ref, lnb_ref, o_ref,
                 u_ref, y_ref, *, rows):
    ts, c = o_ref.shape[1], o_ref.shape[2]
    first = pl.program_id(1) == 0

    def glu(a, b):
        return a.astype(F32) * _sigmoid(b.astype(F32))

    u_ref[:CONV_HALO, :] = jnp.where(first, 0.0, glu(ah_ref[0], bh_ref[0]))

    def glu_strip(i, carry):
        r = pl.multiple_of(i * STRIP, STRIP)
        u_ref[pl.ds(CONV_HALO + r, STRIP), :] = glu(a_ref[0, pl.ds(r, STRIP), :], b_ref[0, pl.ds(r, STRIP), :])
        return carry

    lax.fori_loop(0, ts // STRIP, glu_strip, 0)

    shift = CONV_HALO - (CONV_KERNEL - 1)

    def chunk(ci, carry):
        lanes = pl.ds(pl.multiple_of(ci * LANES, LANES), LANES)
        taps = dw_ref[:, lanes]
        for r0 in range(0, ts, rows):
            acc = jnp.zeros((rows, LANES), F32)
            window = u_ref[pl.ds(r0, rows + CONV_HALO), lanes]
            for phase in range(SUBLANES):
                steps = [a for a in range((CONV_HALO // SUBLANES) + 1)
                         if 0 <= a * SUBLANES + phase - shift < CONV_KERNEL]
                slab = window if phase == 0 else pltpu.roll(window, rows + CONV_HALO - phase, axis=0)
                for a in steps:
                    kk = a * SUBLANES + phase - shift
                    acc = acc + taps[kk:kk + 1, :] * slab[a * SUBLANES:a * SUBLANES + rows, :]
            y_ref[pl.ds(r0, rows), lanes] = acc
        return carry

    lax.fori_loop(0, c // LANES, chunk, 0)

    def norm_strip(i, carry):
        r = pl.multiple_of(i * STRIP, STRIP)
        y = y_ref[pl.ds(r, STRIP), :] + dwb_ref[...]
        mu = jnp.mean(y, axis=-1, keepdims=True)
        yc = y - mu
        var = jnp.mean(yc * yc, axis=-1, keepdims=True)
        z = yc * lax.rsqrt(var + LN_EPS) * lng_ref[...] + lnb_ref[...]
        gate = g_ref[0, pl.ds(r, STRIP), :].astype(F32)
        o_ref[0, pl.ds(r, STRIP), :] = (_silu(z) * _silu(gate)).astype(o_ref.dtype)
        return carry

    lax.fori_loop(0, ts // STRIP, norm_strip, 0, unroll=4)


def conv_module(abg, dw, dw_b, ln_g, ln_b, *, ts=256, rows=128):
    b, s, c3 = abg.shape
    c = c3 // 3
    per = ts // CONV_HALO
    assert s % ts == 0 and ts % CONV_HALO == 0 and ts % rows == 0 and c % LANES == 0
    tile = lambda col: pl.BlockSpec((1, ts, c), lambda bi, i: (bi, i, col))
    halo = lambda col: pl.BlockSpec((1, CONV_HALO, c), lambda bi, i: (bi, jnp.maximum(i * per - 1, 0), col))
    vec = pl.BlockSpec((1, c), lambda bi, i: (0, 0))
    return pl.pallas_call(
        functools.partial(_conv_kernel, rows=rows),
        grid=(b, s // ts),
        in_specs=[tile(0), tile(1), tile(2), halo(0), halo(1),
                  pl.BlockSpec((CONV_KERNEL + 1, c), lambda bi, i: (0, 0)), vec, vec, vec],
        out_specs=pl.BlockSpec((1, ts, c), lambda bi, i: (bi, i, 0)),
        out_shape=jax.ShapeDtypeStruct((b, s, c), BF16),
        scratch_shapes=[pltpu.VMEM((ts + CONV_HALO, c), F32), pltpu.VMEM((ts, c), F32)],
        compiler_params=_params("parallel", "arbitrary"),
        name="conv_module",
    )(abg, abg, abg, abg, abg, dw, dw_b, ln_g, ln_b)


def _out_kernel(y_ref, w_ref, h_ref, g_ref, *out_refs, last):
    h_new = h_ref[...] + jnp.dot(y_ref[...], w_ref[...], preferred_element_type=F32)
    normed = h_new * lax.rsqrt(jnp.mean(h_new * h_new, axis=-1, keepdims=True) + RMS_EPS) * g_ref[...]
    if last:
        out_refs[0][...] = normed
    else:
        out_refs[0][...] = h_new
        out_refs[1][...] = normed.astype(BF16)


def out_projection(y, w, h, g, *, last, tm=256):
    m, k = y.shape
    d = w.shape[1]
    assert m % tm == 0
    row = lambda width: pl.BlockSpec((tm, width), lambda i: (i, 0))
    if last:
        out_specs, out_shape = row(d), jax.ShapeDtypeStruct((m, d), F32)
    else:
        out_specs = [row(d), row(d)]
        out_shape = [jax.ShapeDtypeStruct((m, d), F32), jax.ShapeDtypeStruct((m, d), BF16)]
    return pl.pallas_call(
        functools.partial(_out_kernel, last=last),
        grid=(m // tm,),
        in_specs=[row(k), pl.BlockSpec((k, d), lambda i: (0, 0)), row(d), pl.BlockSpec((1, d), lambda i: (0, 0))],
        out_specs=out_specs,
        out_shape=out_shape,
        compiler_params=_params("parallel"),
        name="out_proj",
    )(y, w, h, g)


def _fox_weights(w_in, b_f):
    w, hh = FOX_WIDTH, FOX_HEADS
    wq = w_in[:, :w] * (FOX_HEAD_DIM ** -0.5 * LOG2E)
    w_qvt = jnp.concatenate([wq, w_in[:, 2 * w:3 * w]], axis=1).T.astype(BF16)
    w_kg = jnp.concatenate([w_in[:, w:2 * w], w_in[:, 3 * w:4 * w]], axis=1).astype(BF16)
    used = (jnp.arange(LANES) % HEAD_LANES) < 3
    head = jnp.arange(LANES) // HEAD_LANES
    wf = jnp.where(used[None, :], w_in[:, 4 * w:][:, head], 0.0).astype(BF16)
    bf = jnp.where(used, b_f[head], 0.0)[None, :].astype(F32)
    return w_qvt, w_kg, wf, bf


def kernel(x, norm_g, fox_w_in, fox_b_f, fox_w_out, conv_w_in, conv_b_in, conv_dw, conv_dw_b, conv_ln_g,
           conv_ln_b, conv_w_out, final_norm_g):
    b, s, d = x.shape
    m = b * s
    h = x.reshape(m, d)
    hn = rmsnorm_bf16(h, norm_g[0][None, :])
    for i in range(DEPTH):
        j = i // 2
        if i % 2 == 0:
            w_qvt, w_kg, wf, bf = _fox_weights(fox_w_in[j], fox_b_f[j])
            hn3 = hn.reshape(b, s, d)
            kg = matmul_nn(hn, w_kg, jnp.zeros((1, w_kg.shape[1]), F32)).reshape(b, s, -1)
            qvt = matmul_nt(w_qvt, hn3)
            caug = forget_gate(hn3, wf, bf)
            y = fox_attention(qvt, kg, caug).reshape(m, FOX_WIDTH)
            w_out = fox_w_out[j].astype(BF16)
        else:
            abg = matmul_nn(hn, conv_w_in[j].astype(BF16), conv_b_in[j][None, :]).reshape(b, s, -1)
            dw = jnp.concatenate([conv_dw[j], jnp.zeros((1, CONV_CHANNELS), F32)], axis=0)
            y = conv_module(abg, dw, conv_dw_b[j][None, :], conv_ln_g[j][None, :],
                            conv_ln_b[j][None, :]).reshape(m, CONV_CHANNELS)
            w_out = conv_w_out[j].astype(BF16)
        if i + 1 < DEPTH:
            h, hn = out_projection(y, w_out, h, norm_g[i + 1][None, :], last=False)
        else:
            out = out_projection(y, w_out, h, final_norm_g[None, :], last=True)
    return out.reshape(b, s, d)
```

```python
import functools
import math

import jax
import jax.numpy as jnp
from jax import lax
from jax.experimental import pallas as pl
from jax.experimental.pallas import tpu as pltpu

D_MODEL = 2048
DEPTH = 4
FOX_HEADS = 16
FOX_HEAD_DIM = 128
FOX_WIDTH = FOX_HEADS * FOX_HEAD_DIM
CONV_CHANNELS = D_MODEL
CONV_KERNEL = 31
RMS_EPS = 1e-6
LN_EPS = 1e-5

LANES = 128
SUBLANES = 8
LOG2E = 1.4426950408889634
VMEM_LIMIT = 56 * 1024 * 1024
HEAD_LANES = LANES // FOX_HEADS
MASK_VALUE = -1e30
CONV_HALO = 32
STRIP = 16

BF16 = jnp.bfloat16
F32 = jnp.float32


def _params(*sem):
    return pltpu.CompilerParams(dimension_semantics=sem, vmem_limit_bytes=VMEM_LIMIT)


def _split3(x):
    hi = x.astype(BF16).astype(F32)
    r = x - hi
    mid = r.astype(BF16).astype(F32)
    lo = (r - mid).astype(BF16).astype(F32)
    return hi, mid, lo


def _sigmoid(x):
    return 0.5 * jnp.tanh(0.5 * x) + 0.5


def _silu(x):
    return x * _sigmoid(x)


def _rmsnorm_kernel(x_ref, g_ref, o_ref):
    x = x_ref[...]
    y = x * lax.rsqrt(jnp.mean(x * x, axis=-1, keepdims=True) + RMS_EPS) * g_ref[...]
    o_ref[...] = y.astype(o_ref.dtype)


def rmsnorm_bf16(x, g, *, tm=512):
    m, d = x.shape
    assert m % tm == 0
    return pl.pallas_call(
        _rmsnorm_kernel,
        grid=(m // tm,),
        in_specs=[pl.BlockSpec((tm, d), lambda i: (i, 0)), pl.BlockSpec((1, d), lambda i: (0, 0))],
        out_specs=pl.BlockSpec((tm, d), lambda i: (i, 0)),
        out_shape=jax.ShapeDtypeStruct((m, d), BF16),
        compiler_params=_params("parallel"),
        name="rmsnorm",
    )(x, g)


def _proj_kernel(x_ref, w_ref, b_ref, o_ref, *, scaled_tiles, scale):
    acc = jnp.dot(x_ref[...], w_ref[...].astype(BF16), preferred_element_type=F32) + b_ref[...]
    if scaled_tiles:
        acc = acc * jnp.where(pl.program_id(1) < scaled_tiles, scale, 1.0)
    o_ref[...] = acc.astype(o_ref.dtype)


def projection(x, w, layer, n_out, bias, *, scaled_cols=0, scale=1.0, tm=1024, tn=1024):
    m, k = x.shape
    assert m % tm == 0 and n_out % tn == 0 and scaled_cols % tn == 0 and n_out <= w.shape[2]
    return pl.pallas_call(
        functools.partial(_proj_kernel, scaled_tiles=scaled_cols // tn, scale=scale),
        grid=(m // tm, n_out // tn),
        in_specs=[
            pl.BlockSpec((tm, k), lambda i, j: (i, 0)),
            pl.BlockSpec((None, k, tn), lambda i, j: (layer, 0, j)),
            pl.BlockSpec((1, tn), lambda i, j: (0, j)),
        ],
        out_specs=pl.BlockSpec((tm, tn), lambda i, j: (i, j)),
        out_shape=jax.ShapeDtypeStruct((m, n_out), BF16),
        compiler_params=_params("parallel", "arbitrary"),
        name="in_proj",
    )(x, w, bias)


def _fgate_kernel(x_ref, wf_ref, bf_ref, o_ref, carry_ref):
    t = x_ref.shape[1]

    @pl.when(pl.program_id(1) == 0)
    def _():
        carry_ref[...] = jnp.zeros_like(carry_ref)

    z = jnp.dot(x_ref[0], wf_ref[...], preferred_element_type=F32) + bf_ref[...]
    log_f = jnp.minimum(z, 0.0) - jnp.log(1.0 + jnp.exp(-jnp.abs(z)))
    row = lax.broadcasted_iota(jnp.int32, (t, t), 0)
    col = lax.broadcasted_iota(jnp.int32, (t, t), 1)
    tri = (col <= row).astype(BF16)
    pieces = jnp.concatenate(_split3(log_f), axis=1).astype(BF16)
    local = jnp.dot(tri, pieces, preferred_element_type=F32)
    c = carry_ref[...] + local[:, :LANES] + local[:, LANES:2 * LANES] + local[:, 2 * LANES:]
    carry_ref[...] = c[t - 1:t, :]
    hi, mid, lo = _split3(c * (-LOG2E))
    piece = lax.broadcasted_iota(jnp.int32, (t, LANES), 1) % HEAD_LANES
    packed = jnp.where(piece == 0, hi, jnp.where(piece == 1, mid, jnp.where(piece == 2, lo, 0.0)))
    o_ref[0] = packed.astype(o_ref.dtype)


def forget_gate(hn, wf, bf, *, t=512):
    b, s, k = hn.shape
    assert s % t == 0
    return pl.pallas_call(
        _fgate_kernel,
        grid=(b, s // t),
        in_specs=[
            pl.BlockSpec((1, t, k), lambda bi, i: (bi, i, 0)),
            pl.BlockSpec((k, LANES), lambda bi, i: (0, 0)),
            pl.BlockSpec((1, LANES), lambda bi, i: (0, 0)),
        ],
        out_specs=pl.BlockSpec((1, t, LANES), lambda bi, i: (bi, i, 0)),
        out_shape=jax.ShapeDtypeStruct((b, s, LANES), BF16),
        scratch_shapes=[pltpu.VMEM((1, LANES), F32)],
        compiler_params=_params("parallel", "arbitrary"),
        name="forget_gate",
    )(hn, wf, bf)


def _attn_kernel(q_ref, k_ref, c_ref, v_ref, g_ref, o_ref,
                 kaug_ref, vaug_ref, qaug_ref, s0_ref, s1_ref, smax0_ref, smax1_ref, acc_ref, m_ref, *, tk):
    h = pl.program_id(1)
    iq = pl.program_id(2)
    tq, d = q_ref.shape[1], q_ref.shape[2]
    assert tq == tk

    @pl.when(iq == 0)
    def _():
        kaug_ref[:, :d] = k_ref[0]
        kaug_ref[:, d:] = c_ref[0]
        vaug_ref[:d, :] = v_ref[0].astype(F32).T.astype(BF16)
        vaug_ref[d:, :] = jnp.ones((vaug_ref.shape[0] - d, vaug_ref.shape[1]), BF16)

    col = lax.broadcasted_iota(jnp.int32, (tq, LANES), 1)
    qaug_ref[:, :d] = q_ref[0]
    qaug_ref[:, d:] = ((col >= h * HEAD_LANES) & (col < h * HEAD_LANES + 3)).astype(BF16)
    acc_ref[...] = jnp.zeros_like(acc_ref)
    m_ref[...] = jnp.full_like(m_ref, MASK_VALUE)
    s_ref, smax_ref = (s0_ref, s1_ref), (smax0_ref, smax1_ref)

    def causal(s, first_key, first_query):
        q_pos = lax.broadcasted_iota(jnp.int32, s.shape, 1) + first_query
        k_pos = lax.broadcasted_iota(jnp.int32, s.shape, 0) + first_key
        return jnp.where(k_pos <= q_pos, s, MASK_VALUE)

    def scores(start):
        return lax.dot_general(kaug_ref[pl.ds(start, tk), :], qaug_ref[...], (((1,), (1,)), ((), ())),
                               preferred_element_type=F32)

    def stage_a(slot, start):
        s = scores(start)
        s_ref[slot][...] = s
        smax_ref[slot][...] = jnp.max(s, axis=0, keepdims=True)

    def stage_bc(slot, start):
        m_old = m_ref[...]
        m_new = jnp.maximum(m_old, smax_ref[slot][...])
        m_ref[...] = m_new
        p = jnp.exp2((s_ref[slot][...] - m_new).astype(BF16))
        pv = jnp.dot(vaug_ref[:, pl.ds(start, tk)], p, preferred_element_type=F32)
        acc_ref[...] = jnp.exp2(m_old - m_new) * acc_ref[...] + pv

    base = pl.multiple_of(iq * tq, tq)

    def key_start(t):
        return pl.multiple_of(jnp.where(t == 0, base, (t - 1) * tk), tk)

    s = causal(scores(base), 0, 0)
    s0_ref[...] = s
    smax0_ref[...] = jnp.max(s, axis=0, keepdims=True)

    def step(t, carry):
        for slot in range(2):
            @pl.when(t % 2 == slot)
            def _():
                stage_a(slot, key_start(t))
                stage_bc(1 - slot, key_start(t - 1))
        return carry

    lax.fori_loop(1, iq + 1, step, 0)

    for slot in range(2):
        @pl.when(iq % 2 == slot)
        def _():
            stage_bc(slot, key_start(iq))

    o = (acc_ref[:d, :] / acc_ref[d:d + 1, :]).T
    gate = g_ref[0].astype(F32)
    o_ref[0] = (o * _silu(gate)).astype(o_ref.dtype)


def fox_attention(qkvg, caug, *, tq=1024, tk=1024):
    b, s, _ = qkvg.shape
    hh, d = FOX_HEADS, FOX_HEAD_DIM
    assert s % tq == 0 and tq == tk
    return pl.pallas_call(
        functools.partial(_attn_kernel, tk=tk),
        grid=(b, hh, s // tq),
        in_specs=[
            pl.BlockSpec((1, tq, d), lambda bi, h, i: (bi, i, h)),
            pl.BlockSpec((1, s, d), lambda bi, h, i: (bi, 0, hh + h)),
            pl.BlockSpec((1, s, LANES), lambda bi, h, i: (bi, 0, 0)),
            pl.BlockSpec((1, s, d), lambda bi, h, i: (bi, 0, 2 * hh + h)),
            pl.BlockSpec((1, tq, d), lambda bi, h, i: (bi, i, 3 * hh + h)),
        ],
        out_specs=pl.BlockSpec((1, tq, d), lambda bi, h, i: (bi, i, h)),
        out_shape=jax.ShapeDtypeStruct((b, s, hh * d), BF16),
        scratch_shapes=[
            pltpu.VMEM((s, d + LANES), BF16),
            pltpu.VMEM((d + STRIP, s), BF16),
            pltpu.VMEM((tq, d + LANES), BF16),
            pltpu.VMEM((tk, tq), F32), pltpu.VMEM((tk, tq), F32),
            pltpu.VMEM((1, tq), F32), pltpu.VMEM((1, tq), F32),
            pltpu.VMEM((d + STRIP, tq), F32),
            pltpu.VMEM((1, tq), F32),
        ],
        compiler_params=_params("parallel", "parallel", "arbitrary"),
        name="fox_attention",
    )(qkvg, qkvg, caug, qkvg, qkvg)


def _conv_kernel(a_ref, b_ref, g_ref, ah_ref, bh_ref, dw_ref, dwb_ref, lng_ref, lnb_ref, o_ref,
                 u_ref, y_ref, *, rows):
    ts, c = o_ref.shape[1], o_ref.shape[2]
    first = pl.program_id(1) == 0

    def glu(a, b):
        return a.astype(F32) * _sigmoid(b.astype(F32))

    u_ref[:CONV_HALO, :] = jnp.where(first, 0.0, glu(ah_ref[0], bh_ref[0]))

    def glu_strip(i, carry):
        r = pl.multiple_of(i * STRIP, STRIP)
        u_ref[pl.ds(CONV_HALO + r, STRIP), :] = glu(a_ref[0, pl.ds(r, STRIP), :], b_ref[0, pl.ds(r, STRIP), :])
        return carry

    lax.fori_loop(0, ts // STRIP, glu_strip, 0)

    shift = CONV_HALO - (CONV_KERNEL - 1)

    def chunk(ci, carry):
        lanes = pl.ds(pl.multiple_of(ci * LANES, LANES), LANES)
        taps = dw_ref[:, lanes]
        for r0 in range(0, ts, rows):
            acc = jnp.zeros((rows, LANES), F32)
            window = u_ref[pl.ds(r0, rows + CONV_HALO), lanes]
            for phase in range(SUBLANES):
                steps = [a for a in range((CONV_HALO // SUBLANES) + 1)
                         if 0 <= a * SUBLANES + phase - shift < CONV_KERNEL]
                slab = window if phase == 0 else pltpu.roll(window, rows + CONV_HALO - phase, axis=0)
                for a in steps:
                    kk = a * SUBLANES + phase - shift
                    acc = acc + taps[kk:kk + 1, :] * slab[a * SUBLANES:a * SUBLANES + rows, :]
            y_ref[pl.ds(r0, rows), lanes] = acc
        return carry

    lax.fori_loop(0, c // LANES, chunk, 0)

    def norm_strip(i, carry):
        r = pl.multiple_of(i * STRIP, STRIP)
        y = y_ref[pl.ds(r, STRIP), :] + dwb_ref[...]
        mu = jnp.mean(y, axis=-1, keepdims=True)
        yc = y - mu
        var = jnp.mean(yc * yc, axis=-1, keepdims=True)
        z = yc * lax.rsqrt(var + LN_EPS) * lng_ref[...] + lnb_ref[...]
        gate = g_ref[0, pl.ds(r, STRIP), :].astype(F32)
        o_ref[0, pl.ds(r, STRIP), :] = (_silu(z) * _silu(gate)).astype(o_ref.dtype)
        return carry

    lax.fori_loop(0, ts // STRIP, norm_strip, 0, unroll=4)


def conv_module(abg, dw, dw_b, ln_g, ln_b, *, ts=256, rows=128):
    b, s, c3 = abg.shape
    c = c3 // 3
    per = ts // CONV_HALO
    assert s % ts == 0 and ts % CONV_HALO == 0 and ts % rows == 0 and c % LANES == 0
    tile = lambda col: pl.BlockSpec((1, ts, c), lambda bi, i: (bi, i, col))
    halo = lambda col: pl.BlockSpec((1, CONV_HALO, c), lambda bi, i: (bi, jnp.maximum(i * per - 1, 0), col))
    vec = pl.BlockSpec((1, c), lambda bi, i: (0, 0))
    return pl.pallas_call(
        functools.partial(_conv_kernel, rows=rows),
        grid=(b, s // ts),
        in_specs=[tile(0), tile(1), tile(2), halo(0), halo(1),
                  pl.BlockSpec((CONV_KERNEL + 1, c), lambda bi, i: (0, 0)), vec, vec, vec],
        out_specs=pl.BlockSpec((1, ts, c), lambda bi, i: (bi, i, 0)),
        out_shape=jax.ShapeDtypeStruct((b, s, c), BF16),
        scratch_shapes=[pltpu.VMEM((ts + CONV_HALO, c), F32), pltpu.VMEM((ts, c), F32)],
        compiler_params=_params("parallel", "arbitrary"),
        name="conv_module",
    )(abg, abg, abg, abg, abg, dw, dw_b, ln_g, ln_b)


def _out_kernel(y_ref, w_ref, h_ref, g_ref, *out_refs, last):
    h_new = h_ref[...] + jnp.dot(y_ref[...], w_ref[...], preferred_element_type=F32)
    normed = h_new * lax.rsqrt(jnp.mean(h_new * h_new, axis=-1, keepdims=True) + RMS_EPS) * g_ref[...]
    if last:
        out_refs[0][...] = normed
    else:
        out_refs[0][...] = h_new
        out_refs[1][...] = normed.astype(BF16)


def out_projection(y, w, h, g, *, last, tm=256):
    m, k = y.shape
    d = w.shape[1]
    assert m % tm == 0
    row = lambda width: pl.BlockSpec((tm, width), lambda i: (i, 0))
    if last:
        out_specs, out_shape = row(d), jax.ShapeDtypeStruct((m, d), F32)
    else:
        out_specs = [row(d), row(d)]
        out_shape = [jax.ShapeDtypeStruct((m, d), F32), jax.ShapeDtypeStruct((m, d), BF16)]
    return pl.pallas_call(
        functools.partial(_out_kernel, last=last),
        grid=(m // tm,),
        in_specs=[row(k), pl.BlockSpec((k, d), lambda i: (0, 0)), row(d), pl.BlockSpec((1, d), lambda i: (0, 0))],
        out_specs=out_specs,
        out_shape=out_shape,
        compiler_params=_params("parallel"),
        name="out_proj",
    )(y, w, h, g)


def _forget_weights(w_in, b_f):
    used = (jnp.arange(LANES) % HEAD_LANES) < 3
    head = jnp.arange(LANES) // HEAD_LANES
    wf = jnp.where(used[None, :], w_in[:, 4 * FOX_WIDTH:][:, head], 0.0).astype(BF16)
    bf = jnp.where(used, b_f[head], 0.0)[None, :].astype(F32)
    return wf, bf


def kernel(x, norm_g, fox_w_in, fox_b_f, fox_w_out, conv_w_in, conv_b_in, conv_dw, conv_dw_b, conv_ln_g,
           conv_ln_b, conv_w_out, final_norm_g):
    b, s, d = x.shape
    m = b * s
    h = x.reshape(m, d)
    hn = rmsnorm_bf16(h, norm_g[0][None, :])
    for i in range(DEPTH):
        j = i // 2
        if i % 2 == 0:
            wf, bf = _forget_weights(fox_w_in[j], fox_b_f[j])
            qkvg = projection(hn, fox_w_in, j, 4 * FOX_WIDTH, jnp.zeros((1, 4 * FOX_WIDTH), F32),
                              scaled_cols=FOX_WIDTH, scale=FOX_HEAD_DIM ** -0.5 * LOG2E)
            caug = forget_gate(hn.reshape(b, s, d), wf, bf)
            y = fox_attention(qkvg.reshape(b, s, -1), caug).reshape(m, FOX_WIDTH)
            w_out = fox_w_out[j].astype(BF16)
        else:
            abg = projection(hn, conv_w_in, j, 3 * CONV_CHANNELS, conv_b_in[j][None, :]).reshape(b, s, -1)
            dw = jnp.concatenate([conv_dw[j], jnp.zeros((1, CONV_CHANNELS), F32)], axis=0)
            y = conv_module(abg, dw, conv_dw_b[j][None, :], conv_ln_g[j][None, :],
                            conv_ln_b[j][None, :]).reshape(m, CONV_CHANNELS)
            w_out = conv_w_out[j].astype(BF16)
        if i + 1 < DEPTH:
            h, hn = out_projection(y, w_out, h, norm_g[i + 1][None, :], last=False)
        else:
            out = out_projection(y, w_out, h, final_norm_g[None, :], last=True)
    return out.reshape(b, s, d)
```

```python
import functools
import math

import jax
import jax.numpy as jnp
from jax import lax
from jax.experimental import pallas as pl
from jax.experimental.pallas import tpu as pltpu

D_MODEL = 2048
DEPTH = 4
FOX_HEADS = 16
FOX_HEAD_DIM = 128
FOX_WIDTH = FOX_HEADS * FOX_HEAD_DIM
CONV_CHANNELS = D_MODEL
CONV_KERNEL = 31
RMS_EPS = 1e-6
LN_EPS = 1e-5

LANES = 128
SUBLANES = 8
LOG2E = 1.4426950408889634
VMEM_LIMIT = 56 * 1024 * 1024
HEAD_LANES = LANES // FOX_HEADS
MASK_VALUE = -1e30
CONV_HALO = 32
STRIP = 16

BF16 = jnp.bfloat16
F32 = jnp.float32


def _params(*sem):
    return pltpu.CompilerParams(dimension_semantics=sem, vmem_limit_bytes=VMEM_LIMIT)


def _split3(x):
    hi = x.astype(BF16).astype(F32)
    r = x - hi
    mid = r.astype(BF16).astype(F32)
    lo = (r - mid).astype(BF16).astype(F32)
    return hi, mid, lo


def _sigmoid(x):
    return 0.5 * jnp.tanh(0.5 * x) + 0.5


def _silu(x):
    return x * _sigmoid(x)


def _rmsnorm_kernel(x_ref, g_ref, o_ref):
    x = x_ref[...]
    y = x * lax.rsqrt(jnp.mean(x * x, axis=-1, keepdims=True) + RMS_EPS) * g_ref[...]
    o_ref[...] = y.astype(o_ref.dtype)


def rmsnorm_bf16(x, g, *, tm=512):
    m, d = x.shape
    assert m % tm == 0
    return pl.pallas_call(
        _rmsnorm_kernel,
        grid=(m // tm,),
        in_specs=[pl.BlockSpec((tm, d), lambda i: (i, 0)), pl.BlockSpec((1, d), lambda i: (0, 0))],
        out_specs=pl.BlockSpec((tm, d), lambda i: (i, 0)),
        out_shape=jax.ShapeDtypeStruct((m, d), BF16),
        compiler_params=_params("parallel"),
        name="rmsnorm",
    )(x, g)


def _proj_kernel(x_ref, w_ref, b_ref, o_ref, *, scaled_tiles, scale):
    acc = jnp.dot(x_ref[...], w_ref[...].astype(BF16), preferred_element_type=F32) + b_ref[...]
    if scaled_tiles:
        acc = acc * jnp.where(pl.program_id(1) < scaled_tiles, scale, 1.0)
    o_ref[...] = acc.astype(o_ref.dtype)


def projection(x, w, layer, n_out, bias, *, scaled_cols=0, scale=1.0, tm=1024, tn=1024):
    m, k = x.shape
    assert m % tm == 0 and n_out % tn == 0 and scaled_cols % tn == 0 and n_out <= w.shape[2]
    return pl.pallas_call(
        functools.partial(_proj_kernel, scaled_tiles=scaled_cols // tn, scale=scale),
        grid=(m // tm, n_out // tn),
        in_specs=[
            pl.BlockSpec((tm, k), lambda i, j: (i, 0)),
            pl.BlockSpec((None, k, tn), lambda i, j: (layer, 0, j)),
            pl.BlockSpec((1, tn), lambda i, j: (0, j)),
        ],
        out_specs=pl.BlockSpec((tm, tn), lambda i, j: (i, j)),
        out_shape=jax.ShapeDtypeStruct((m, n_out), BF16),
        compiler_params=_params("parallel", "arbitrary"),
        name="in_proj",
    )(x, w, bias)


def _fgate_kernel(x_ref, wf_ref, bf_ref, o_ref, carry_ref):
    t = x_ref.shape[1]

    @pl.when(pl.program_id(1) == 0)
    def _():
        carry_ref[...] = jnp.zeros_like(carry_ref)

    z = jnp.dot(x_ref[0], wf_ref[...], preferred_element_type=F32) + bf_ref[...]
    log_f = jnp.minimum(z, 0.0) - jnp.log(1.0 + jnp.exp(-jnp.abs(z)))
    row = lax.broadcasted_iota(jnp.int32, (t, t), 0)
    col = lax.broadcasted_iota(jnp.int32, (t, t), 1)
    tri = (col <= row).astype(BF16)
    pieces = jnp.concatenate(_split3(log_f), axis=1).astype(BF16)
    local = jnp.dot(tri, pieces, preferred_element_type=F32)
    c = carry_ref[...] + local[:, :LANES] + local[:, LANES:2 * LANES] + local[:, 2 * LANES:]
    carry_ref[...] = c[t - 1:t, :]
    hi, mid, lo = _split3(c * (-LOG2E))
    piece = lax.broadcasted_iota(jnp.int32, (t, LANES), 1) % HEAD_LANES
    packed = jnp.where(piece == 0, hi, jnp.where(piece == 1, mid, jnp.where(piece == 2, lo, 0.0)))
    o_ref[0] = packed.astype(o_ref.dtype)


def forget_gate(hn, wf, bf, *, t=512):
    b, s, k = hn.shape
    assert s % t == 0
    return pl.pallas_call(
        _fgate_kernel,
        grid=(b, s // t),
        in_specs=[
            pl.BlockSpec((1, t, k), lambda bi, i: (bi, i, 0)),
            pl.BlockSpec((k, LANES), lambda bi, i: (0, 0)),
            pl.BlockSpec((1, LANES), lambda bi, i: (0, 0)),
        ],
        out_specs=pl.BlockSpec((1, t, LANES), lambda bi, i: (bi, i, 0)),
        out_shape=jax.ShapeDtypeStruct((b, s, LANES), BF16),
        scratch_shapes=[pltpu.VMEM((1, LANES), F32)],
        compiler_params=_params("parallel", "arbitrary"),
        name="forget_gate",
    )(hn, wf, bf)


def _attn_kernel(q_ref, k_ref, c_ref, v_ref, g_ref, o_ref,
                 kaug_ref, vaug_ref, qaug_ref, s0_ref, s1_ref, smax0_ref, smax1_ref, acc_ref, m_ref, *, tk):
    h = pl.program_id(1)
    iq = pl.program_id(2)
    tq, d = q_ref.shape[1], q_ref.shape[2]
    assert tq == tk

    @pl.when(iq == 0)
    def _():
        kaug_ref[:, :d] = k_ref[0]
        kaug_ref[:, d:] = c_ref[0]
        vaug_ref[:d, :] = v_ref[0].astype(F32).T.astype(BF16)
        vaug_ref[d:, :] = jnp.ones((vaug_ref.shape[0] - d, vaug_ref.shape[1]), BF16)

    col = lax.broadcasted_iota(jnp.int32, (tq, LANES), 1)
    qaug_ref[:, :d] = q_ref[0]
    qaug_ref[:, d:] = ((col >= h * HEAD_LANES) & (col < h * HEAD_LANES + 3)).astype(BF16)
    acc_ref[...] = jnp.zeros_like(acc_ref)
    m_ref[...] = jnp.full_like(m_ref, MASK_VALUE)
    s_ref, smax_ref = (s0_ref, s1_ref), (smax0_ref, smax1_ref)

    def causal(s, first_key, first_query):
        q_pos = lax.broadcasted_iota(jnp.int32, s.shape, 1) + first_query
        k_pos = lax.broadcasted_iota(jnp.int32, s.shape, 0) + first_key
        return jnp.where(k_pos <= q_pos, s, MASK_VALUE)

    def scores(start):
        return lax.dot_general(kaug_ref[pl.ds(start, tk), :], qaug_ref[...], (((1,), (1,)), ((), ())),
                               preferred_element_type=F32)

    def stage_a(slot, start):
        s = scores(start)
        s_ref[slot][...] = s
        smax_ref[slot][...] = jnp.max(s, axis=0, keepdims=True)

    def stage_bc(slot, start):
        m_old = m_ref[...]
        m_new = jnp.maximum(m_old, smax_ref[slot][...])
        m_ref[...] = m_new
        p = jnp.exp2((s_ref[slot][...] - m_new).astype(BF16))
        pv = jnp.dot(vaug_ref[:, pl.ds(start, tk)], p, preferred_element_type=F32)
        acc_ref[...] = jnp.exp2(m_old - m_new) * acc_ref[...] + pv

    base = pl.multiple_of(iq * tq, tq)

    def key_start(t):
        return pl.multiple_of(jnp.where(t == 0, base, (t - 1) * tk), tk)

    hk = tk // 2
    nt = (((1,), (1,)), ((), ()))
    top = lax.dot_general(kaug_ref[pl.ds(base, hk), :], qaug_ref[...], nt, preferred_element_type=F32)
    top_left = causal(top[:, :hk], 0, 0)
    bottom = causal(lax.dot_general(kaug_ref[pl.ds(base + hk, hk), :], qaug_ref[hk:, :], nt,
                                    preferred_element_type=F32), 0, 0)
    s0_ref[:hk, :hk] = top_left
    s0_ref[:hk, hk:] = top[:, hk:]
    s0_ref[hk:, :hk] = jnp.full((hk, hk), MASK_VALUE, F32)
    s0_ref[hk:, hk:] = bottom
    smax0_ref[:, :hk] = jnp.max(top_left, axis=0, keepdims=True)
    smax0_ref[:, hk:] = jnp.maximum(jnp.max(top[:, hk:], axis=0, keepdims=True),
                                    jnp.max(bottom, axis=0, keepdims=True))

    def step(t, carry):
        for slot in range(2):
            @pl.when(t % 2 == slot)
            def _():
                stage_a(slot, key_start(t))
                stage_bc(1 - slot, key_start(t - 1))
        return carry

    lax.fori_loop(1, iq + 1, step, 0)

    for slot in range(2):
        @pl.when(iq % 2 == slot)
        def _():
            stage_bc(slot, key_start(iq))

    o = (acc_ref[:d, :] / acc_ref[d:d + 1, :]).T
    gate = g_ref[0].astype(F32)
    o_ref[0] = (o * _silu(gate)).astype(o_ref.dtype)


def fox_attention(qkvg, caug, *, tq=1024, tk=1024):
    b, s, _ = qkvg.shape
    hh, d = FOX_HEADS, FOX_HEAD_DIM
    assert s % tq == 0 and tq == tk
    return pl.pallas_call(
        functools.partial(_attn_kernel, tk=tk),
        grid=(b, hh, s // tq),
        in_specs=[
            pl.BlockSpec((1, tq, d), lambda bi, h, i: (bi, i, h)),
            pl.BlockSpec((1, s, d), lambda bi, h, i: (bi, 0, hh + h)),
            pl.BlockSpec((1, s, LANES), lambda bi, h, i: (bi, 0, 0)),
            pl.BlockSpec((1, s, d), lambda bi, h, i: (bi, 0, 2 * hh + h)),
            pl.BlockSpec((1, tq, d), lambda bi, h, i: (bi, i, 3 * hh + h)),
        ],
        out_specs=pl.BlockSpec((1, tq, d), lambda bi, h, i: (bi, i, h)),
        out_shape=jax.ShapeDtypeStruct((b, s, hh * d), BF16),
        scratch_shapes=[
            pltpu.VMEM((s, d + LANES), BF16),
            pltpu.VMEM((d + STRIP, s), BF16),
            pltpu.VMEM((tq, d + LANES), BF16),
            pltpu.VMEM((tk, tq), F32), pltpu.VMEM((tk, tq), F32),
            pltpu.VMEM((1, tq), F32), pltpu.VMEM((1, tq), F32),
            pltpu.VMEM((d + STRIP, tq), F32),
            pltpu.VMEM((1, tq), F32),
        ],
        compiler_params=_params("parallel", "parallel", "arbitrary"),
        name="fox_attention",
    )(qkvg, qkvg, caug, qkvg, qkvg)


def _conv_kernel(a_ref, b_ref, g_ref, ah_ref, bh_ref, dw_ref, dwb_ref, lng_ref, lnb_ref, o_ref,
                 u_ref, y_ref, *, rows):
    ts, c = o_ref.shape[1], o_ref.shape[2]
    first = pl.program_id(1) == 0

    def glu(a, b):
        return a.astype(F32) * _sigmoid(b.astype(F32))

    u_ref[:CONV_HALO, :] = jnp.where(first, 0.0, glu(ah_ref[0], bh_ref[0]))

    def glu_strip(i, carry):
        r = pl.multiple_of(i * STRIP, STRIP)
        u_ref[pl.ds(CONV_HALO + r, STRIP), :] = glu(a_ref[0, pl.ds(r, STRIP), :], b_ref[0, pl.ds(r, STRIP), :])
        return carry

    lax.fori_loop(0, ts // STRIP, glu_strip, 0)

    shift = CONV_HALO - (CONV_KERNEL - 1)

    def chunk(ci, carry):
        lanes = pl.ds(pl.multiple_of(ci * LANES, LANES), LANES)
        taps = dw_ref[:, lanes]
        for r0 in range(0, ts, rows):
            acc = jnp.zeros((rows, LANES), F32)
            window = u_ref[pl.ds(r0, rows + CONV_HALO), lanes]
            for phase in range(SUBLANES):
                steps = [a for a in range((CONV_HALO // SUBLANES) + 1)
                         if 0 <= a * SUBLANES + phase - shift < CONV_KERNEL]
                slab = window if phase == 0 else pltpu.roll(window, rows + CONV_HALO - phase, axis=0)
                for a in steps:
                    kk = a * SUBLANES + phase - shift
                    acc = acc + taps[kk:kk + 1, :] * slab[a * SUBLANES:a * SUBLANES + rows, :]
            y_ref[pl.ds(r0, rows), lanes] = acc
        return carry

    lax.fori_loop(0, c // LANES, chunk, 0)

    def norm_strip(i, carry):
        r = pl.multiple_of(i * STRIP, STRIP)
        y = y_ref[pl.ds(r, STRIP), :] + dwb_ref[...]
        mu = jnp.mean(y, axis=-1, keepdims=True)
        yc = y - mu
        var = jnp.mean(yc * yc, axis=-1, keepdims=True)
        z = yc * lax.rsqrt(var + LN_EPS) * lng_ref[...] + lnb_ref[...]
        gate = g_ref[0, pl.ds(r, STRIP), :].astype(F32)
        o_ref[0, pl.ds(r, STRIP), :] = (_silu(z) * _silu(gate)).astype(o_ref.dtype)
        return carry

    lax.fori_loop(0, ts // STRIP, norm_strip, 0, unroll=4)


def conv_module(abg, dw, dw_b, ln_g, ln_b, *, ts=256, rows=128):
    b, s, c3 = abg.shape
    c = c3 // 3
    per = ts // CONV_HALO
    assert s % ts == 0 and ts % CONV_HALO == 0 and ts % rows == 0 and c % LANES == 0
    tile = lambda col: pl.BlockSpec((1, ts, c), lambda bi, i: (bi, i, col))
    halo = lambda col: pl.BlockSpec((1, CONV_HALO, c), lambda bi, i: (bi, jnp.maximum(i * per - 1, 0), col))
    vec = pl.BlockSpec((1, c), lambda bi, i: (0, 0))
    return pl.pallas_call(
        functools.partial(_conv_kernel, rows=rows),
        grid=(b, s // ts),
        in_specs=[tile(0), tile(1), tile(2), halo(0), halo(1),
                  pl.BlockSpec((CONV_KERNEL + 1, c), lambda bi, i: (0, 0)), vec, vec, vec],
        out_specs=pl.BlockSpec((1, ts, c), lambda bi, i: (bi, i, 0)),
        out_shape=jax.ShapeDtypeStruct((b, s, c), BF16),
        scratch_shapes=[pltpu.VMEM((ts + CONV_HALO, c), F32), pltpu.VMEM((ts, c), F32)],
        compiler_params=_params("parallel", "arbitrary"),
        name="conv_module",
    )(abg, abg, abg, abg, abg, dw, dw_b, ln_g, ln_b)


def _out_kernel(y_ref, w_ref, h_ref, g_ref, *out_refs, last):
    h_new = h_ref[...] + jnp.dot(y_ref[...], w_ref[...], preferred_element_type=F32)
    normed = h_new * lax.rsqrt(jnp.mean(h_new * h_new, axis=-1, keepdims=True) + RMS_EPS) * g_ref[...]
    if last:
        out_refs[0][...] = normed
    else:
        out_refs[0][...] = h_new
        out_refs[1][...] = normed.astype(BF16)


def out_projection(y, w, h, g, *, last, tm=512):
    m, k = y.shape
    d = w.shape[1]
    assert m % tm == 0
    row = lambda width: pl.BlockSpec((tm, width), lambda i: (i, 0))
    if last:
        out_specs, out_shape = row(d), jax.ShapeDtypeStruct((m, d), F32)
    else:
        out_specs = [row(d), row(d)]
        out_shape = [jax.ShapeDtypeStruct((m, d), F32), jax.ShapeDtypeStruct((m, d), BF16)]
    return pl.pallas_call(
        functools.partial(_out_kernel, last=last),
        grid=(m // tm,),
        in_specs=[row(k), pl.BlockSpec((k, d), lambda i: (0, 0)), row(d), pl.BlockSpec((1, d), lambda i: (0, 0))],
        out_specs=out_specs,
        out_shape=out_shape,
        compiler_params=_params("parallel"),
        name="out_proj",
    )(y, w, h, g)


def _forget_weights(w_in, b_f):
    used = (jnp.arange(LANES) % HEAD_LANES) < 3
    head = jnp.arange(LANES) // HEAD_LANES
    w_f = lax.optimization_barrier(w_in[:, 4 * FOX_WIDTH:])
    wf = jnp.where(used[None, :], w_f[:, head], 0.0).astype(BF16)
    bf = jnp.where(used, b_f[head], 0.0)[None, :].astype(F32)
    return wf, bf


def kernel(x, norm_g, fox_w_in, fox_b_f, fox_w_out, conv_w_in, conv_b_in, conv_dw, conv_dw_b, conv_ln_g,
           conv_ln_b, conv_w_out, final_norm_g):
    b, s, d = x.shape
    m = b * s
    h = x.reshape(m, d)
    hn = rmsnorm_bf16(h, norm_g[0][None, :])
    fox_w = fox_w_in[:, :, :4 * FOX_WIDTH].astype(BF16)
    for i in range(DEPTH):
        j = i // 2
        if i % 2 == 0:
            wf, bf = _forget_weights(fox_w_in[j], fox_b_f[j])
            qkvg = projection(hn, fox_w, j, 4 * FOX_WIDTH, jnp.zeros((1, 4 * FOX_WIDTH), F32),
                              scaled_cols=FOX_WIDTH, scale=FOX_HEAD_DIM ** -0.5 * LOG2E)
            caug = forget_gate(hn.reshape(b, s, d), wf, bf)
            y = fox_attention(qkvg.reshape(b, s, -1), caug).reshape(m, FOX_WIDTH)
            w_out = fox_w_out[j].astype(BF16)
        else:
            abg = projection(hn, conv_w_in, j, 3 * CONV_CHANNELS, conv_b_in[j][None, :]).reshape(b, s, -1)
            dw = jnp.concatenate([conv_dw[j], jnp.zeros((1, CONV_CHANNELS), F32)], axis=0)
            y = conv_module(abg, dw, conv_dw_b[j][None, :], conv_ln_g[j][None, :],
                            conv_ln_b[j][None, :]).reshape(m, CONV_CHANNELS)
            w_out = conv_w_out[j].astype(BF16)
        if i + 1 < DEPTH:
            h, hn = out_projection(y, w_out, h, norm_g[i + 1][None, :], last=False)
        else:
            out = out_projection(y, w_out, h, final_norm_g[None, :], last=True)
    return out.reshape(b, s, d)
```

```python
import functools
import math

import jax
import jax.numpy as jnp
from jax import lax
from jax.experimental import pallas as pl
from jax.experimental.pallas import tpu as pltpu

D_MODEL = 2048
DEPTH = 4
FOX_HEADS = 16
FOX_HEAD_DIM = 128
FOX_WIDTH = FOX_HEADS * FOX_HEAD_DIM
CONV_CHANNELS = D_MODEL
CONV_KERNEL = 31
RMS_EPS = 1e-6
LN_EPS = 1e-5

LANES = 128
SUBLANES = 8
LOG2E = 1.4426950408889634
VMEM_LIMIT = 56 * 1024 * 1024
HEAD_LANES = LANES // FOX_HEADS
MASK_VALUE = -1e30
CONV_HALO = 32
STRIP = 16

BF16 = jnp.bfloat16
F32 = jnp.float32


def _params(*sem):
    return pltpu.CompilerParams(dimension_semantics=sem, vmem_limit_bytes=VMEM_LIMIT)


def _split3(x):
    hi = x.astype(BF16).astype(F32)
    r = x - hi
    mid = r.astype(BF16).astype(F32)
    lo = (r - mid).astype(BF16).astype(F32)
    return hi, mid, lo


def _sigmoid(x):
    return 0.5 * jnp.tanh(0.5 * x) + 0.5


def _silu(x):
    half = 0.5 * x
    return half * (jnp.tanh(half) + 1.0)


def _rmsnorm_kernel(x_ref, g_ref, o_ref):
    x = x_ref[...]
    y = x * lax.rsqrt(jnp.mean(x * x, axis=-1, keepdims=True) + RMS_EPS) * g_ref[...]
    o_ref[...] = y.astype(o_ref.dtype)


def rmsnorm_bf16(x, g, *, tm=512):
    m, d = x.shape
    assert m % tm == 0
    return pl.pallas_call(
        _rmsnorm_kernel,
        grid=(m // tm,),
        in_specs=[pl.BlockSpec((tm, d), lambda i: (i, 0)), pl.BlockSpec((1, d), lambda i: (0, 0))],
        out_specs=pl.BlockSpec((tm, d), lambda i: (i, 0)),
        out_shape=jax.ShapeDtypeStruct((m, d), BF16),
        compiler_params=_params("parallel"),
        name="rmsnorm",
    )(x, g)


def _proj_kernel(x_ref, w_ref, b_ref, o_ref, *, scaled_tiles, scale):
    acc = jnp.dot(x_ref[...], w_ref[...].astype(BF16), preferred_element_type=F32) + b_ref[...]
    if scaled_tiles:
        acc = acc * jnp.where(pl.program_id(1) < scaled_tiles, scale, 1.0)
    o_ref[...] = acc.astype(o_ref.dtype)


def projection(x, w, layer, n_out, bias, *, scaled_cols=0, scale=1.0, tm=1024, tn=1024):
    m, k = x.shape
    assert m % tm == 0 and n_out % tn == 0 and scaled_cols % tn == 0 and n_out <= w.shape[1]
    assert w.shape[0] % k == 0 and layer < w.shape[0] // k
    return pl.pallas_call(
        functools.partial(_proj_kernel, scaled_tiles=scaled_cols // tn, scale=scale),
        grid=(m // tm, n_out // tn),
        in_specs=[
            pl.BlockSpec((tm, k), lambda i, j: (i, 0)),
            pl.BlockSpec((k, tn), lambda i, j: (layer, j)),
            pl.BlockSpec((1, tn), lambda i, j: (0, j)),
        ],
        out_specs=pl.BlockSpec((tm, tn), lambda i, j: (i, j)),
        out_shape=jax.ShapeDtypeStruct((m, n_out), BF16),
        compiler_params=_params("parallel", "arbitrary"),
        name="in_proj",
    )(x, w, bias)


def _fgate_kernel(x_ref, wf_ref, bf_ref, o_ref, carry_ref):
    t = x_ref.shape[1]

    @pl.when(pl.program_id(1) == 0)
    def _():
        carry_ref[...] = jnp.zeros_like(carry_ref)

    z = jnp.dot(x_ref[0], wf_ref[...], preferred_element_type=F32) + bf_ref[...]
    log_f = jnp.minimum(z, 0.0) - jnp.log(1.0 + jnp.exp(-jnp.abs(z)))
    row = lax.broadcasted_iota(jnp.int32, (t, t), 0)
    col = lax.broadcasted_iota(jnp.int32, (t, t), 1)
    tri = (col <= row).astype(BF16)
    pieces = jnp.concatenate(_split3(log_f), axis=1).astype(BF16)
    local = jnp.dot(tri, pieces, preferred_element_type=F32)
    c = carry_ref[...] + local[:, :LANES] + local[:, LANES:2 * LANES] + local[:, 2 * LANES:]
    carry_ref[...] = c[t - 1:t, :]
    hi, mid, lo = _split3(c * (-LOG2E))
    piece = lax.broadcasted_iota(jnp.int32, (t, LANES), 1) % HEAD_LANES
    packed = jnp.where(piece == 0, hi, jnp.where(piece == 1, mid, jnp.where(piece == 2, lo, 0.0)))
    o_ref[0] = packed.astype(o_ref.dtype)


def forget_gate(hn, wf, bf, *, t=512):
    b, s, k = hn.shape
    assert s % t == 0
    return pl.pallas_call(
        _fgate_kernel,
        grid=(b, s // t),
        in_specs=[
            pl.BlockSpec((1, t, k), lambda bi, i: (bi, i, 0)),
            pl.BlockSpec((k, LANES), lambda bi, i: (0, 0)),
            pl.BlockSpec((1, LANES), lambda bi, i: (0, 0)),
        ],
        out_specs=pl.BlockSpec((1, t, LANES), lambda bi, i: (bi, i, 0)),
        out_shape=jax.ShapeDtypeStruct((b, s, LANES), BF16),
        scratch_shapes=[pltpu.VMEM((1, LANES), F32)],
        compiler_params=_params("parallel", "arbitrary"),
        name="forget_gate",
    )(hn, wf, bf)


def _attn_kernel(q_ref, k_ref, c_ref, v_ref, g_ref, o_ref,
                 kaug_ref, vaug_ref, qaug_ref, s0_ref, s1_ref, smax0_ref, smax1_ref, acc_ref, m_ref, *, tk):
    h = pl.program_id(1)
    iq = pl.program_id(2)
    tq, d = q_ref.shape[1], q_ref.shape[2]
    assert tq == tk

    @pl.when(iq == 0)
    def _():
        kaug_ref[:, :d] = k_ref[0]
        kaug_ref[:, d:] = c_ref[0]
        vaug_ref[:d, :] = v_ref[0].astype(F32).T.astype(BF16)
        vaug_ref[d:, :] = jnp.ones((vaug_ref.shape[0] - d, vaug_ref.shape[1]), BF16)

    col = lax.broadcasted_iota(jnp.int32, (tq, LANES), 1)
    qaug_ref[:, :d] = q_ref[0]
    qaug_ref[:, d:] = ((col >= h * HEAD_LANES) & (col < h * HEAD_LANES + 3)).astype(BF16)
    acc_ref[...] = jnp.zeros_like(acc_ref)
    m_ref[...] = jnp.full_like(m_ref, MASK_VALUE)
    s_ref, smax_ref = (s0_ref, s1_ref), (smax0_ref, smax1_ref)

    def causal(s, first_key, first_query):
        q_pos = lax.broadcasted_iota(jnp.int32, s.shape, 1) + first_query
        k_pos = lax.broadcasted_iota(jnp.int32, s.shape, 0) + first_key
        return jnp.where(k_pos <= q_pos, s, MASK_VALUE)

    def scores(start):
        return lax.dot_general(kaug_ref[pl.ds(start, tk), :], qaug_ref[...], (((1,), (1,)), ((), ())),
                               preferred_element_type=F32)

    def stage_a(slot, start):
        s = scores(start)
        s_ref[slot][...] = s
        smax_ref[slot][...] = jnp.max(s, axis=0, keepdims=True)

    def stage_bc(slot, start):
        m_old = m_ref[...]
        m_new = jnp.maximum(m_old, smax_ref[slot][...])
        m_ref[...] = m_new
        p = jnp.exp2((s_ref[slot][...] - m_new).astype(BF16))
        pv = jnp.dot(vaug_ref[:, pl.ds(start, tk)], p, preferred_element_type=F32)
        acc_ref[...] = jnp.exp2(m_old - m_new) * acc_ref[...] + pv

    base = pl.multiple_of(iq * tq, tq)

    def key_start(t):
        return pl.multiple_of(jnp.where(t == 0, base, (t - 1) * tk), tk)

    hk = tk // 2
    nt = (((1,), (1,)), ((), ()))
    top = lax.dot_general(kaug_ref[pl.ds(base, hk), :], qaug_ref[...], nt, preferred_element_type=F32)
    top_left = causal(top[:, :hk], 0, 0)
    bottom = causal(lax.dot_general(kaug_ref[pl.ds(base + hk, hk), :], qaug_ref[hk:, :], nt,
                                    preferred_element_type=F32), 0, 0)
    s0_ref[:hk, :hk] = top_left
    s0_ref[:hk, hk:] = top[:, hk:]
    s0_ref[hk:, :hk] = jnp.full((hk, hk), MASK_VALUE, F32)
    s0_ref[hk:, hk:] = bottom
    smax0_ref[:, :hk] = jnp.max(top_left, axis=0, keepdims=True)
    smax0_ref[:, hk:] = jnp.maximum(jnp.max(top[:, hk:], axis=0, keepdims=True),
                                    jnp.max(bottom, axis=0, keepdims=True))

    def step(t, carry):
        for slot in range(2):
            @pl.when(t % 2 == slot)
            def _():
                stage_a(slot, key_start(t))
                stage_bc(1 - slot, key_start(t - 1))
        return carry

    lax.fori_loop(1, iq + 1, step, 0)

    for slot in range(2):
        @pl.when(iq % 2 == slot)
        def _():
            stage_bc(slot, key_start(iq))

    o = (acc_ref[:d, :] / acc_ref[d:d + 1, :]).T
    gate = g_ref[0].astype(F32)
    o_ref[0] = (o * _silu(gate)).astype(o_ref.dtype)


def fox_attention(qkvg, caug, *, tq=1024, tk=1024):
    b, s, _ = qkvg.shape
    hh, d = FOX_HEADS, FOX_HEAD_DIM
    assert s % tq == 0 and tq == tk
    return pl.pallas_call(
        functools.partial(_attn_kernel, tk=tk),
        grid=(b, hh, s // tq),
        in_specs=[
            pl.BlockSpec((1, tq, d), lambda bi, h, i: (bi, i, h)),
            pl.BlockSpec((1, s, d), lambda bi, h, i: (bi, 0, hh + h)),
            pl.BlockSpec((1, s, LANES), lambda bi, h, i: (bi, 0, 0)),
            pl.BlockSpec((1, s, d), lambda bi, h, i: (bi, 0, 2 * hh + h)),
            pl.BlockSpec((1, tq, d), lambda bi, h, i: (bi, i, 3 * hh + h)),
        ],
        out_specs=pl.BlockSpec((1, tq, d), lambda bi, h, i: (bi, i, h)),
        out_shape=jax.ShapeDtypeStruct((b, s, hh * d), BF16),
        scratch_shapes=[
            pltpu.VMEM((s, d + LANES), BF16),
            pltpu.VMEM((d + STRIP, s), BF16),
            pltpu.VMEM((tq, d + LANES), BF16),
            pltpu.VMEM((tk, tq), F32), pltpu.VMEM((tk, tq), F32),
            pltpu.VMEM((1, tq), F32), pltpu.VMEM((1, tq), F32),
            pltpu.VMEM((d + STRIP, tq), F32),
            pltpu.VMEM((1, tq), F32),
        ],
        compiler_params=_params("parallel", "parallel", "arbitrary"),
        name="fox_attention",
    )(qkvg, qkvg, caug, qkvg, qkvg)


def _conv_kernel(a_ref, b_ref, g_ref, ah_ref, bh_ref, dw_ref, dwb_ref, lng_ref, lnb_ref, o_ref,
                 u_ref, y_ref, *, rows):
    ts, c = o_ref.shape[1], o_ref.shape[2]
    first = pl.program_id(1) == 0

    def glu(a, b):
        return a.astype(F32) * _sigmoid(b.astype(F32))

    u_ref[:CONV_HALO, :] = jnp.where(first, 0.0, glu(ah_ref[0], bh_ref[0]))

    def glu_strip(i, carry):
        r = pl.multiple_of(i * STRIP, STRIP)
        u_ref[pl.ds(CONV_HALO + r, STRIP), :] = glu(a_ref[0, pl.ds(r, STRIP), :], b_ref[0, pl.ds(r, STRIP), :])
        return carry

    lax.fori_loop(0, ts // STRIP, glu_strip, 0)

    shift = CONV_HALO - (CONV_KERNEL - 1)

    def chunk(ci, carry):
        lanes = pl.ds(pl.multiple_of(ci * LANES, LANES), LANES)
        taps = dw_ref[:, lanes]
        for r0 in range(0, ts, rows):
            acc = jnp.zeros((rows, LANES), F32)
            window = u_ref[pl.ds(r0, rows + CONV_HALO), lanes]
            for phase in range(SUBLANES):
                steps = [a for a in range((CONV_HALO // SUBLANES) + 1)
                         if 0 <= a * SUBLANES + phase - shift < CONV_KERNEL]
                slab = window if phase == 0 else pltpu.roll(window, rows + CONV_HALO - phase, axis=0)
                for a in steps:
                    kk = a * SUBLANES + phase - shift
                    acc = acc + taps[kk:kk + 1, :] * slab[a * SUBLANES:a * SUBLANES + rows, :]
            y_ref[pl.ds(r0, rows), lanes] = acc
        return carry

    lax.fori_loop(0, c // LANES, chunk, 0)

    def norm_strip(i, carry):
        r = pl.multiple_of(i * STRIP, STRIP)
        y = y_ref[pl.ds(r, STRIP), :] + dwb_ref[...]
        mu = jnp.mean(y, axis=-1, keepdims=True)
        yc = y - mu
        var = jnp.mean(yc * yc, axis=-1, keepdims=True)
        z = yc * lax.rsqrt(var + LN_EPS) * lng_ref[...] + lnb_ref[...]
        gate = g_ref[0, pl.ds(r, STRIP), :].astype(F32)
        o_ref[0, pl.ds(r, STRIP), :] = (_silu(z) * _silu(gate)).astype(o_ref.dtype)
        return carry

    lax.fori_loop(0, ts // STRIP, norm_strip, 0, unroll=4)


def conv_module(abg, dw, dw_b, ln_g, ln_b, *, ts=256, rows=128):
    b, s, c3 = abg.shape
    c = c3 // 3
    per = ts // CONV_HALO
    assert s % ts == 0 and ts % CONV_HALO == 0 and ts % rows == 0 and c % LANES == 0
    tile = lambda col: pl.BlockSpec((1, ts, c), lambda bi, i: (bi, i, col))
    halo = lambda col: pl.BlockSpec((1, CONV_HALO, c), lambda bi, i: (bi, jnp.maximum(i * per - 1, 0), col))
    vec = pl.BlockSpec((1, c), lambda bi, i: (0, 0))
    return pl.pallas_call(
        functools.partial(_conv_kernel, rows=rows),
        grid=(b, s // ts),
        in_specs=[tile(0), tile(1), tile(2), halo(0), halo(1),
                  pl.BlockSpec((CONV_KERNEL + 1, c), lambda bi, i: (0, 0)), vec, vec, vec],
        out_specs=pl.BlockSpec((1, ts, c), lambda bi, i: (bi, i, 0)),
        out_shape=jax.ShapeDtypeStruct((b, s, c), BF16),
        scratch_shapes=[pltpu.VMEM((ts + CONV_HALO, c), F32), pltpu.VMEM((ts, c), F32)],
        compiler_params=_params("parallel", "arbitrary"),
        name="conv_module",
    )(abg, abg, abg, abg, abg, dw, dw_b, ln_g, ln_b)


def _out_kernel(y_ref, w_ref, h_ref, g_ref, *out_refs, last):
    h_new = h_ref[...] + jnp.dot(y_ref[...], w_ref[...], preferred_element_type=F32)
    normed = h_new * lax.rsqrt(jnp.mean(h_new * h_new, axis=-1, keepdims=True) + RMS_EPS) * g_ref[...]
    if last:
        out_refs[0][...] = normed
    else:
        out_refs[0][...] = h_new
        out_refs[1][...] = normed.astype(BF16)


def out_projection(y, w, layer, h, g, *, last, tm=512):
    m, k = y.shape
    d = w.shape[1]
    assert m % tm == 0 and w.shape[0] % k == 0 and layer < w.shape[0] // k
    row = lambda width: pl.BlockSpec((tm, width), lambda i: (i, 0))
    if last:
        out_specs, out_shape = row(d), jax.ShapeDtypeStruct((m, d), F32)
    else:
        out_specs = [row(d), row(d)]
        out_shape = [jax.ShapeDtypeStruct((m, d), F32), jax.ShapeDtypeStruct((m, d), BF16)]
    return pl.pallas_call(
        functools.partial(_out_kernel, last=last),
        grid=(m // tm,),
        in_specs=[row(k), pl.BlockSpec((k, d), lambda i: (layer, 0)), row(d),
                  pl.BlockSpec((1, d), lambda i: (0, 0))],
        out_specs=out_specs,
        out_shape=out_shape,
        compiler_params=_params("parallel"),
        name="out_proj",
    )(y, w, h, g)


def _forget_weights(w_f, b_f):
    lane = jnp.arange(LANES)
    spread = ((lane[None, :] // HEAD_LANES == jnp.arange(FOX_HEADS)[:, None])
              & (lane[None, :] % HEAD_LANES < 3)).astype(F32)
    wf = jnp.dot(w_f, spread, precision=lax.Precision.HIGHEST).astype(BF16)
    bf = jnp.dot(b_f[None, :], spread, precision=lax.Precision.HIGHEST)
    return wf, bf


def kernel(x, norm_g, fox_w_in, fox_b_f, fox_w_out, conv_w_in, conv_b_in, conv_dw, conv_dw_b, conv_ln_g,
           conv_ln_b, conv_w_out, final_norm_g):
    b, s, d = x.shape
    m = b * s
    h = x.reshape(m, d)
    hn = rmsnorm_bf16(h, norm_g[0][None, :])
    fox_w = fox_w_in[:, :, :4 * FOX_WIDTH].astype(BF16).reshape(-1, 4 * FOX_WIDTH)
    fox_wf = fox_w_in[:, :, 4 * FOX_WIDTH:]
    conv_w = conv_w_in.reshape(-1, conv_w_in.shape[-1])
    fox_w_o = fox_w_out.astype(BF16).reshape(-1, d)
    conv_w_o = conv_w_out.astype(BF16).reshape(-1, d)
    for i in range(DEPTH):
        j = i // 2
        if i % 2 == 0:
            wf, bf = _forget_weights(fox_wf[j], fox_b_f[j])
            qkvg = projection(hn, fox_w, j, 4 * FOX_WIDTH, jnp.zeros((1, 4 * FOX_WIDTH), F32),
                              scaled_cols=FOX_WIDTH, scale=FOX_HEAD_DIM ** -0.5 * LOG2E)
            caug = forget_gate(hn.reshape(b, s, d), wf, bf)
            y = fox_attention(qkvg.reshape(b, s, -1), caug).reshape(m, FOX_WIDTH)
            w_out = fox_w_o
        else:
            abg = projection(hn, conv_w, j, 3 * CONV_CHANNELS, conv_b_in[j][None, :]).reshape(b, s, -1)
            dw = jnp.concatenate([conv_dw[j], jnp.zeros((1, CONV_CHANNELS), F32)], axis=0)
            y = conv_module(abg, dw, conv_dw_b[j][None, :], conv_ln_g[j][None, :],
                            conv_ln_b[j][None, :]).reshape(m, CONV_CHANNELS)
            w_out = conv_w_o
        if i + 1 < DEPTH:
            h, hn = out_projection(y, w_out, j, h, norm_g[i + 1][None, :], last=False)
        else:
            out = out_projection(y, w_out, j, h, final_norm_g[None, :], last=True)
    return out.reshape(b, s, d)
```

```python
import functools
import math

import jax
import jax.numpy as jnp
from jax import lax
from jax.experimental import pallas as pl
from jax.experimental.pallas import tpu as pltpu

D_MODEL = 2048
DEPTH = 4
FOX_HEADS = 16
FOX_HEAD_DIM = 128
FOX_WIDTH = FOX_HEADS * FOX_HEAD_DIM
CONV_CHANNELS = D_MODEL
CONV_KERNEL = 31
RMS_EPS = 1e-6
LN_EPS = 1e-5

LANES = 128
SUBLANES = 8
LOG2E = 1.4426950408889634
VMEM_LIMIT = 56 * 1024 * 1024
HEAD_LANES = LANES // FOX_HEADS
MASK_VALUE = -1e30
CONV_HALO = 32
STRIP = 16

BF16 = jnp.bfloat16
F32 = jnp.float32


def _params(*sem):
    return pltpu.CompilerParams(dimension_semantics=sem, vmem_limit_bytes=VMEM_LIMIT)


def _split3(x):
    hi = x.astype(BF16).astype(F32)
    r = x - hi
    mid = r.astype(BF16).astype(F32)
    lo = (r - mid).astype(BF16).astype(F32)
    return hi, mid, lo


def _sigmoid(x):
    return 0.5 * jnp.tanh(0.5 * x) + 0.5


def _silu(x):
    half = 0.5 * x
    return half * (jnp.tanh(half) + 1.0)


def _rmsnorm_kernel(x_ref, g_ref, o_ref):
    x = x_ref[...]
    y = x * lax.rsqrt(jnp.mean(x * x, axis=-1, keepdims=True) + RMS_EPS) * g_ref[...]
    o_ref[...] = y.astype(o_ref.dtype)


def rmsnorm_bf16(x, g, *, tm=512):
    m, d = x.shape
    assert m % tm == 0
    return pl.pallas_call(
        _rmsnorm_kernel,
        grid=(m // tm,),
        in_specs=[pl.BlockSpec((tm, d), lambda i: (i, 0)), pl.BlockSpec((1, d), lambda i: (0, 0))],
        out_specs=pl.BlockSpec((tm, d), lambda i: (i, 0)),
        out_shape=jax.ShapeDtypeStruct((m, d), BF16),
        compiler_params=_params("parallel"),
        name="rmsnorm",
    )(x, g)


def _proj_kernel(x_ref, w_ref, b_ref, o_ref, *, scaled_tiles, scale):
    acc = jnp.dot(x_ref[...], w_ref[...].astype(BF16), preferred_element_type=F32) + b_ref[...]
    if scaled_tiles:
        acc = acc * jnp.where(pl.program_id(1) < scaled_tiles, scale, 1.0)
    o_ref[...] = acc.astype(o_ref.dtype)


def projection(x, w, layer, n_out, bias, *, scaled_cols=0, scale=1.0, tm=1024, tn=1024):
    m, k = x.shape
    assert m % tm == 0 and n_out % tn == 0 and scaled_cols % tn == 0 and n_out <= w.shape[1]
    assert w.shape[0] % k == 0 and layer < w.shape[0] // k
    return pl.pallas_call(
        functools.partial(_proj_kernel, scaled_tiles=scaled_cols // tn, scale=scale),
        grid=(m // tm, n_out // tn),
        in_specs=[
            pl.BlockSpec((tm, k), lambda i, j: (i, 0)),
            pl.BlockSpec((k, tn), lambda i, j: (layer, j)),
            pl.BlockSpec((1, tn), lambda i, j: (0, j)),
        ],
        out_specs=pl.BlockSpec((tm, tn), lambda i, j: (i, j)),
        out_shape=jax.ShapeDtypeStruct((m, n_out), BF16),
        compiler_params=_params("parallel", "arbitrary"),
        name="in_proj",
    )(x, w, bias)


def _fgate_kernel(x_ref, wf_ref, bf_ref, o_ref, carry_ref):
    t = x_ref.shape[1]

    @pl.when(pl.program_id(1) == 0)
    def _():
        carry_ref[...] = jnp.zeros_like(carry_ref)

    z = jnp.dot(x_ref[0], wf_ref[...], preferred_element_type=F32) + bf_ref[...]
    log_f = jnp.minimum(z, 0.0) - jnp.log(1.0 + jnp.exp(-jnp.abs(z)))
    row = lax.broadcasted_iota(jnp.int32, (t, t), 0)
    col = lax.broadcasted_iota(jnp.int32, (t, t), 1)
    tri = (col <= row).astype(BF16)
    pieces = jnp.concatenate(_split3(log_f), axis=1).astype(BF16)
    local = jnp.dot(tri, pieces, preferred_element_type=F32)
    c = carry_ref[...] + local[:, :LANES] + local[:, LANES:2 * LANES] + local[:, 2 * LANES:]
    carry_ref[...] = c[t - 1:t, :]
    hi, mid, lo = _split3(c * (-LOG2E))
    piece = lax.broadcasted_iota(jnp.int32, (t, LANES), 1) % HEAD_LANES
    packed = jnp.where(piece == 0, hi, jnp.where(piece == 1, mid, jnp.where(piece == 2, lo, 0.0)))
    o_ref[0] = packed.astype(o_ref.dtype)


def forget_gate(hn, wf, bf, *, t=512):
    b, s, k = hn.shape
    assert s % t == 0
    return pl.pallas_call(
        _fgate_kernel,
        grid=(b, s // t),
        in_specs=[
            pl.BlockSpec((1, t, k), lambda bi, i: (bi, i, 0)),
            pl.BlockSpec((k, LANES), lambda bi, i: (0, 0)),
            pl.BlockSpec((1, LANES), lambda bi, i: (0, 0)),
        ],
        out_specs=pl.BlockSpec((1, t, LANES), lambda bi, i: (bi, i, 0)),
        out_shape=jax.ShapeDtypeStruct((b, s, LANES), BF16),
        scratch_shapes=[pltpu.VMEM((1, LANES), F32)],
        compiler_params=_params("parallel", "arbitrary"),
        name="forget_gate",
    )(hn, wf, bf)


def _attn_kernel(q_ref, k_ref, c_ref, v_ref, g_ref, o_ref,
                 kaug_ref, vaug_ref, qaug_ref, s0_ref, s1_ref, smax0_ref, smax1_ref, acc_ref, m_ref, *, tk):
    h = pl.program_id(1)
    iq = pl.program_id(2)
    tq, d = q_ref.shape[1], q_ref.shape[2]
    assert tq == tk

    @pl.when(iq == 0)
    def _():
        kaug_ref[:, :d] = k_ref[0]
        kaug_ref[:, d:] = c_ref[0]
        vaug_ref[:d, :] = v_ref[0].astype(F32).T.astype(BF16)
        vaug_ref[d:, :] = jnp.ones((vaug_ref.shape[0] - d, vaug_ref.shape[1]), BF16)
        col = lax.broadcasted_iota(jnp.int32, (tq, LANES), 1)
        qaug_ref[:, d:] = ((col >= h * HEAD_LANES) & (col < h * HEAD_LANES + 3)).astype(BF16)

    qaug_ref[:, :d] = q_ref[0]
    acc_ref[...] = jnp.zeros_like(acc_ref)
    m_ref[...] = jnp.full_like(m_ref, MASK_VALUE)
    s_ref, smax_ref = (s0_ref, s1_ref), (smax0_ref, smax1_ref)

    def causal(s, first_key, first_query):
        q_pos = lax.broadcasted_iota(jnp.int32, s.shape, 1) + first_query
        k_pos = lax.broadcasted_iota(jnp.int32, s.shape, 0) + first_key
        return jnp.where(k_pos <= q_pos, s, MASK_VALUE)

    def scores(start):
        return lax.dot_general(kaug_ref[pl.ds(start, tk), :], qaug_ref[...], (((1,), (1,)), ((), ())),
                               preferred_element_type=F32)

    def stage_a(slot, start):
        s = scores(start)
        s_ref[slot][...] = s
        smax_ref[slot][...] = jnp.max(s, axis=0, keepdims=True)

    def stage_bc(slot, start):
        m_old = m_ref[...]
        m_new = jnp.maximum(m_old, smax_ref[slot][...])
        m_ref[...] = m_new
        p = jnp.exp2((s_ref[slot][...] - m_new).astype(BF16))
        pv = jnp.dot(vaug_ref[:, pl.ds(start, tk)], p, preferred_element_type=F32)
        acc_ref[...] = jnp.exp2(m_old - m_new) * acc_ref[...] + pv

    base = pl.multiple_of(iq * tq, tq)

    def key_start(t):
        return pl.multiple_of(jnp.where(t == 0, base, (t - 1) * tk), tk)

    hk = tk // 2
    nt = (((1,), (1,)), ((), ()))
    top = lax.dot_general(kaug_ref[pl.ds(base, hk), :], qaug_ref[...], nt, preferred_element_type=F32)
    top_left = causal(top[:, :hk], 0, 0)
    bottom = causal(lax.dot_general(kaug_ref[pl.ds(base + hk, hk), :], qaug_ref[hk:, :], nt,
                                    preferred_element_type=F32), 0, 0)
    s0_ref[:hk, :hk] = top_left
    s0_ref[:hk, hk:] = top[:, hk:]
    s0_ref[hk:, :hk] = jnp.full((hk, hk), MASK_VALUE, F32)
    s0_ref[hk:, hk:] = bottom
    smax0_ref[:, :hk] = jnp.max(top_left, axis=0, keepdims=True)
    smax0_ref[:, hk:] = jnp.maximum(jnp.max(top[:, hk:], axis=0, keepdims=True),
                                    jnp.max(bottom, axis=0, keepdims=True))

    def step(t, carry):
        for slot in range(2):
            @pl.when(t % 2 == slot)
            def _():
                stage_a(slot, key_start(t))
                stage_bc(1 - slot, key_start(t - 1))
        return carry

    lax.fori_loop(1, iq + 1, step, 0)

    for slot in range(2):
        @pl.when(iq % 2 == slot)
        def _():
            stage_bc(slot, key_start(iq))

    o = (acc_ref[:d, :] / acc_ref[d:d + 1, :]).T
    gate = g_ref[0].astype(F32)
    o_ref[0] = (o * _silu(gate)).astype(o_ref.dtype)


def fox_attention(qkvg, caug, *, tq=1024, tk=1024):
    b, s, _ = qkvg.shape
    hh, d = FOX_HEADS, FOX_HEAD_DIM
    assert s % tq == 0 and tq == tk
    return pl.pallas_call(
        functools.partial(_attn_kernel, tk=tk),
        grid=(b, hh, s // tq),
        in_specs=[
            pl.BlockSpec((1, tq, d), lambda bi, h, i: (bi, i, h)),
            pl.BlockSpec((1, s, d), lambda bi, h, i: (bi, 0, hh + h)),
            pl.BlockSpec((1, s, LANES), lambda bi, h, i: (bi, 0, 0)),
            pl.BlockSpec((1, s, d), lambda bi, h, i: (bi, 0, 2 * hh + h)),
            pl.BlockSpec((1, tq, d), lambda bi, h, i: (bi, i, 3 * hh + h)),
        ],
        out_specs=pl.BlockSpec((1, tq, d), lambda bi, h, i: (bi, i, h)),
        out_shape=jax.ShapeDtypeStruct((b, s, hh * d), BF16),
        scratch_shapes=[
            pltpu.VMEM((s, d + LANES), BF16),
            pltpu.VMEM((d + STRIP, s), BF16),
            pltpu.VMEM((tq, d + LANES), BF16),
            pltpu.VMEM((tk, tq), F32), pltpu.VMEM((tk, tq), F32),
            pltpu.VMEM((1, tq), F32), pltpu.VMEM((1, tq), F32),
            pltpu.VMEM((d + STRIP, tq), F32),
            pltpu.VMEM((1, tq), F32),
        ],
        compiler_params=_params("parallel", "parallel", "arbitrary"),
        name="fox_attention",
    )(qkvg, qkvg, caug, qkvg, qkvg)


def _conv_kernel(a_ref, b_ref, g_ref, ah_ref, bh_ref, dw_ref, dwb_ref, lng_ref, lnb_ref, o_ref,
                 u_ref, y_ref, *, rows):
    ts, c = o_ref.shape[1], o_ref.shape[2]
    first = pl.program_id(1) == 0

    def glu(a, b):
        return a.astype(F32) * _sigmoid(b.astype(F32))

    n_chunks = c // LANES

    def put_u(row, value):
        for ci in range(n_chunks):
            u_ref[ci, pl.ds(row, value.shape[0]), :] = value[:, ci * LANES:(ci + 1) * LANES]

    put_u(0, jnp.where(first, 0.0, glu(ah_ref[0], bh_ref[0])))

    def glu_strip(i, carry):
        r = pl.multiple_of(i * STRIP, STRIP)
        put_u(CONV_HALO + r, glu(a_ref[0, pl.ds(r, STRIP), :], b_ref[0, pl.ds(r, STRIP), :]))
        return carry

    lax.fori_loop(0, ts // STRIP, glu_strip, 0)

    shift = CONV_HALO - (CONV_KERNEL - 1)

    def chunk(ci, carry):
        lanes = pl.ds(pl.multiple_of(ci * LANES, LANES), LANES)
        taps = dw_ref[:, lanes]
        for r0 in range(0, ts, rows):
            u = [u_ref[ci, pl.ds(r0 + shift + m, SUBLANES, stride=SUBLANES), :]
                 for m in range(rows // SUBLANES + CONV_KERNEL - 1)]
            for j in range(rows // SUBLANES):
                acc = taps[0:1, :] * u[j]
                for kk in range(1, CONV_KERNEL):
                    acc = acc + taps[kk:kk + 1, :] * u[j + kk]
                y_ref[ci, pl.ds(r0 + j, SUBLANES, stride=SUBLANES), :] = acc
        return carry

    lax.fori_loop(0, n_chunks, chunk, 0)

    def norm_strip(i, carry):
        r = pl.multiple_of(i * STRIP, STRIP)
        y = jnp.concatenate([y_ref[ci, pl.ds(r, STRIP), :] for ci in range(n_chunks)], axis=1) + dwb_ref[...]
        mu = jnp.mean(y, axis=-1, keepdims=True)
        yc = y - mu
        var = jnp.mean(yc * yc, axis=-1, keepdims=True)
        z = yc * lax.rsqrt(var + LN_EPS) * lng_ref[...] + lnb_ref[...]
        gate = g_ref[0, pl.ds(r, STRIP), :].astype(F32)
        o_ref[0, pl.ds(r, STRIP), :] = (_silu(z) * _silu(gate)).astype(o_ref.dtype)
        return carry

    lax.fori_loop(0, ts // STRIP, norm_strip, 0, unroll=4)


def conv_module(abg, dw, dw_b, ln_g, ln_b, *, ts=256):
    b, s, c3 = abg.shape
    c = c3 // 3
    per = ts // CONV_HALO
    rows = SUBLANES * SUBLANES
    assert s % ts == 0 and ts % CONV_HALO == 0 and ts % rows == 0 and c % LANES == 0
    tile = lambda col: pl.BlockSpec((1, ts, c), lambda bi, i: (bi, i, col))
    halo = lambda col: pl.BlockSpec((1, CONV_HALO, c), lambda bi, i: (bi, jnp.maximum(i * per - 1, 0), col))
    vec = pl.BlockSpec((1, c), lambda bi, i: (0, 0))
    return pl.pallas_call(
        functools.partial(_conv_kernel, rows=rows),
        grid=(b, s // ts),
        in_specs=[tile(0), tile(1), tile(2), halo(0), halo(1),
                  pl.BlockSpec((CONV_KERNEL + 1, c), lambda bi, i: (0, 0)), vec, vec, vec],
        out_specs=pl.BlockSpec((1, ts, c), lambda bi, i: (bi, i, 0)),
        out_shape=jax.ShapeDtypeStruct((b, s, c), BF16),
        scratch_shapes=[pltpu.VMEM((c // LANES, ts + CONV_HALO, LANES), F32),
                        pltpu.VMEM((c // LANES, ts, LANES), F32)],
        compiler_params=_params("parallel", "arbitrary"),
        name="conv_module",
    )(abg, abg, abg, abg, abg, dw, dw_b, ln_g, ln_b)


def _out_kernel(y_ref, w_ref, h_ref, g_ref, *out_refs, last):
    h_new = h_ref[...] + jnp.dot(y_ref[...], w_ref[...], preferred_element_type=F32)
    normed = h_new * lax.rsqrt(jnp.mean(h_new * h_new, axis=-1, keepdims=True) + RMS_EPS) * g_ref[...]
    if last:
        out_refs[0][...] = normed
    else:
        out_refs[0][...] = h_new
        out_refs[1][...] = normed.astype(BF16)


def out_projection(y, w, layer, h, g, *, last, tm=512):
    m, k = y.shape
    d = w.shape[1]
    assert m % tm == 0 and w.shape[0] % k == 0 and layer < w.shape[0] // k
    row = lambda width: pl.BlockSpec((tm, width), lambda i: (i, 0))
    if last:
        out_specs, out_shape = row(d), jax.ShapeDtypeStruct((m, d), F32)
    else:
        out_specs = [row(d), row(d)]
        out_shape = [jax.ShapeDtypeStruct((m, d), F32), jax.ShapeDtypeStruct((m, d), BF16)]
    return pl.pallas_call(
        functools.partial(_out_kernel, last=last),
        grid=(m // tm,),
        in_specs=[row(k), pl.BlockSpec((k, d), lambda i: (layer, 0)), row(d),
                  pl.BlockSpec((1, d), lambda i: (0, 0))],
        out_specs=out_specs,
        out_shape=out_shape,
        compiler_params=_params("parallel"),
        name="out_proj",
    )(y, w, h, g)


def _forget_weights(w_f, b_f):
    lane = jnp.arange(LANES)
    spread = ((lane[None, :] // HEAD_LANES == jnp.arange(FOX_HEADS)[:, None])
              & (lane[None, :] % HEAD_LANES < 3)).astype(F32)
    wf = jnp.dot(w_f, spread, precision=lax.Precision.HIGHEST).astype(BF16)
    bf = jnp.dot(b_f[None, :], spread, precision=lax.Precision.HIGHEST)
    return wf, bf


def kernel(x, norm_g, fox_w_in, fox_b_f, fox_w_out, conv_w_in, conv_b_in, conv_dw, conv_dw_b, conv_ln_g,
           conv_ln_b, conv_w_out, final_norm_g):
    b, s, d = x.shape
    m = b * s
    h = x.reshape(m, d)
    hn = rmsnorm_bf16(h, norm_g[0][None, :])
    fox_w = fox_w_in[:, :, :4 * FOX_WIDTH].astype(BF16).reshape(-1, 4 * FOX_WIDTH)
    fox_wf = fox_w_in[:, :, 4 * FOX_WIDTH:]
    conv_w = conv_w_in.reshape(-1, conv_w_in.shape[-1])
    fox_w_o = fox_w_out.astype(BF16).reshape(-1, d)
    conv_w_o = conv_w_out.astype(BF16).reshape(-1, d)
    for i in range(DEPTH):
        j = i // 2
        if i % 2 == 0:
            wf, bf = _forget_weights(fox_wf[j], fox_b_f[j])
            qkvg = projection(hn, fox_w, j, 4 * FOX_WIDTH, jnp.zeros((1, 4 * FOX_WIDTH), F32),
                              scaled_cols=FOX_WIDTH, scale=FOX_HEAD_DIM ** -0.5 * LOG2E)
            caug = forget_gate(hn.reshape(b, s, d), wf, bf)
            y = fox_attention(qkvg.reshape(b, s, -1), caug).reshape(m, FOX_WIDTH)
            w_out = fox_w_o
        else:
            abg = projection(hn, conv_w, j, 3 * CONV_CHANNELS, conv_b_in[j][None, :]).reshape(b, s, -1)
            dw = jnp.concatenate([conv_dw[j], jnp.zeros((1, CONV_CHANNELS), F32)], axis=0)
            y = conv_module(abg, dw, conv_dw_b[j][None, :], conv_ln_g[j][None, :],
                            conv_ln_b[j][None, :]).reshape(m, CONV_CHANNELS)
            w_out = conv_w_o
        if i + 1 < DEPTH:
            h, hn = out_projection(y, w_out, j, h, norm_g[i + 1][None, :], last=False)
        else:
            out = out_projection(y, w_out, j, h, final_norm_g[None, :], last=True)
    return out.reshape(b, s, d)
```

```python
import functools
import math

import jax
import jax.numpy as jnp
from jax import lax
from jax.experimental import pallas as pl
from jax.experimental.pallas import tpu as pltpu

D_MODEL = 2048
DEPTH = 4
FOX_HEADS = 16
FOX_HEAD_DIM = 128
FOX_WIDTH = FOX_HEADS * FOX_HEAD_DIM
CONV_CHANNELS = D_MODEL
CONV_KERNEL = 31
RMS_EPS = 1e-6
LN_EPS = 1e-5

LANES = 128
SUBLANES = 8
LOG2E = 1.4426950408889634
VMEM_LIMIT = 56 * 1024 * 1024
HEAD_LANES = LANES // FOX_HEADS
MASK_VALUE = -1e30
CONV_HALO = 32
STRIP = 16

BF16 = jnp.bfloat16
F32 = jnp.float32


def _params(*sem):
    return pltpu.CompilerParams(dimension_semantics=sem, vmem_limit_bytes=VMEM_LIMIT)


def _split3(x):
    hi = x.astype(BF16).astype(F32)
    r = x - hi
    mid = r.astype(BF16).astype(F32)
    lo = (r - mid).astype(BF16).astype(F32)
    return hi, mid, lo


def _sigmoid(x):
    return 0.5 * jnp.tanh(0.5 * x) + 0.5


def _silu(x):
    half = 0.5 * x
    return half * (jnp.tanh(half) + 1.0)


def _rmsnorm_kernel(x_ref, g_ref, o_ref):
    x = x_ref[...]
    y = x * lax.rsqrt(jnp.mean(x * x, axis=-1, keepdims=True) + RMS_EPS) * g_ref[...]
    o_ref[...] = y.astype(o_ref.dtype)


def rmsnorm_bf16(x, g, *, tm=512):
    m, d = x.shape
    assert m % tm == 0
    return pl.pallas_call(
        _rmsnorm_kernel,
        grid=(m // tm,),
        in_specs=[pl.BlockSpec((tm, d), lambda i: (i, 0)), pl.BlockSpec((1, d), lambda i: (0, 0))],
        out_specs=pl.BlockSpec((tm, d), lambda i: (i, 0)),
        out_shape=jax.ShapeDtypeStruct((m, d), BF16),
        compiler_params=_params("parallel"),
        name="rmsnorm",
    )(x, g)


def _proj_kernel(x_ref, w_ref, b_ref, o_ref, *, scaled_tiles, scale):
    acc = jnp.dot(x_ref[...], w_ref[...].astype(BF16), preferred_element_type=F32) + b_ref[...]
    if scaled_tiles:
        acc = acc * jnp.where(pl.program_id(1) < scaled_tiles, scale, 1.0)
    o_ref[...] = acc.astype(o_ref.dtype)


def projection(x, w, layer, n_out, bias, *, scaled_cols=0, scale=1.0, tm=1024, tn=1024):
    m, k = x.shape
    assert m % tm == 0 and n_out % tn == 0 and scaled_cols % tn == 0 and n_out <= w.shape[1]
    assert w.shape[0] % k == 0 and layer < w.shape[0] // k
    return pl.pallas_call(
        functools.partial(_proj_kernel, scaled_tiles=scaled_cols // tn, scale=scale),
        grid=(m // tm, n_out // tn),
        in_specs=[
            pl.BlockSpec((tm, k), lambda i, j: (i, 0)),
            pl.BlockSpec((k, tn), lambda i, j: (layer, j)),
            pl.BlockSpec((1, tn), lambda i, j: (0, j)),
        ],
        out_specs=pl.BlockSpec((tm, tn), lambda i, j: (i, j)),
        out_shape=jax.ShapeDtypeStruct((m, n_out), BF16),
        compiler_params=_params("parallel", "arbitrary"),
        name="in_proj",
    )(x, w, bias)


def _fgate_kernel(x_ref, wf_ref, bf_ref, o_ref, carry_ref):
    t = x_ref.shape[1]

    @pl.when(pl.program_id(1) == 0)
    def _():
        carry_ref[...] = jnp.zeros_like(carry_ref)

    z = jnp.dot(x_ref[0], wf_ref[...], preferred_element_type=F32) + bf_ref[...]
    log_f = jnp.minimum(z, 0.0) - jnp.log(1.0 + jnp.exp(-jnp.abs(z)))
    row = lax.broadcasted_iota(jnp.int32, (t, t), 0)
    col = lax.broadcasted_iota(jnp.int32, (t, t), 1)
    tri = (col <= row).astype(BF16)
    pieces = jnp.concatenate(_split3(log_f), axis=1).astype(BF16)
    local = jnp.dot(tri, pieces, preferred_element_type=F32)
    c = carry_ref[...] + local[:, :LANES] + local[:, LANES:2 * LANES] + local[:, 2 * LANES:]
    carry_ref[...] = c[t - 1:t, :]
    hi, mid, lo = _split3(c * (-LOG2E))
    piece = lax.broadcasted_iota(jnp.int32, (t, LANES), 1) % HEAD_LANES
    packed = jnp.where(piece == 0, hi, jnp.where(piece == 1, mid, jnp.where(piece == 2, lo, 0.0)))
    o_ref[0] = packed.astype(o_ref.dtype)


def forget_gate(hn, wf, bf, *, t=512):
    b, s, k = hn.shape
    assert s % t == 0
    return pl.pallas_call(
        _fgate_kernel,
        grid=(b, s // t),
        in_specs=[
            pl.BlockSpec((1, t, k), lambda bi, i: (bi, i, 0)),
            pl.BlockSpec((k, LANES), lambda bi, i: (0, 0)),
            pl.BlockSpec((1, LANES), lambda bi, i: (0, 0)),
        ],
        out_specs=pl.BlockSpec((1, t, LANES), lambda bi, i: (bi, i, 0)),
        out_shape=jax.ShapeDtypeStruct((b, s, LANES), BF16),
        scratch_shapes=[pltpu.VMEM((1, LANES), F32)],
        compiler_params=_params("parallel", "arbitrary"),
        name="forget_gate",
    )(hn, wf, bf)


def _attn_kernel(q_ref, k_ref, c_ref, v_ref, g_ref, o_ref,
                 kaug_ref, vaug_ref, qaug_ref, s0_ref, s1_ref, smax0_ref, smax1_ref, acc_ref, m_ref, *, tk):
    h = pl.program_id(1)
    s_len, d = q_ref.shape[1], q_ref.shape[2]
    tq = tk
    hk = tk // 2
    nq = s_len // tq
    n_pairs = nq * (nq + 1) // 2
    nt = (((1,), (1,)), ((), ()))

    kaug_ref[:, :d] = k_ref[0]
    kaug_ref[:, d:] = c_ref[0]
    vaug_ref[:d, :] = v_ref[0].astype(F32).T.astype(BF16)
    vaug_ref[d:, :] = jnp.ones((vaug_ref.shape[0] - d, s_len), BF16)
    col = lax.broadcasted_iota(jnp.int32, (s_len, LANES), 1)
    qaug_ref[:, :d] = q_ref[0]
    qaug_ref[:, d:] = ((col >= h * HEAD_LANES) & (col < h * HEAD_LANES + 3)).astype(BF16)
    acc_ref[...] = jnp.zeros_like(acc_ref)
    s_ref, smax_ref = (s0_ref, s1_ref), (smax0_ref, smax1_ref)

    def causal(s):
        q_pos = lax.broadcasted_iota(jnp.int32, s.shape, 1)
        k_pos = lax.broadcasted_iota(jnp.int32, s.shape, 0)
        return jnp.where(k_pos <= q_pos, s, MASK_VALUE)

    def scores(k_start, k_rows, q_start, q_rows):
        return lax.dot_general(kaug_ref[pl.ds(k_start, k_rows), :], qaug_ref[pl.ds(q_start, q_rows), :], nt,
                               preferred_element_type=F32)

    def stage_a(slot, iq, t):
        s = scores(pl.multiple_of((t - 1) * tk, tk), tk, pl.multiple_of(iq * tq, tq), tq)
        s_ref[slot][...] = s
        smax_ref[slot][...] = jnp.max(s, axis=0, keepdims=True)

    def stage_a_diag(slot, iq):
        base = pl.multiple_of(iq * tq, tq)
        top = scores(base, hk, base, tq)
        top_left = causal(top[:, :hk])
        bottom = causal(scores(base + hk, hk, base + hk, hk))
        s_ref[slot][:hk, :hk] = top_left
        s_ref[slot][:hk, hk:] = top[:, hk:]
        s_ref[slot][hk:, :hk] = jnp.full((hk, hk), MASK_VALUE, F32)
        s_ref[slot][hk:, hk:] = bottom
        smax_ref[slot][:, :hk] = jnp.max(top_left, axis=0, keepdims=True)
        smax_ref[slot][:, hk:] = jnp.maximum(jnp.max(top[:, hk:], axis=0, keepdims=True),
                                             jnp.max(bottom, axis=0, keepdims=True))

    def stage_bc(slot, iq, t):
        k_start = pl.multiple_of(jnp.where(t == 0, iq * tq, (t - 1) * tk), tk)
        m_old = jnp.where(t == 0, MASK_VALUE, m_ref[...])
        m_new = jnp.maximum(m_old, smax_ref[slot][...])
        m_ref[...] = m_new
        p = jnp.exp2((s_ref[slot][...] - m_new).astype(BF16))
        pv = jnp.dot(vaug_ref[:, pl.ds(k_start, tk)], p, preferred_element_type=F32)
        acc_ref[...] = jnp.exp2(m_old - m_new) * acc_ref[...] + pv

    def finalize(iq):
        rows = pl.ds(pl.multiple_of(iq * tq, tq), tq)
        o = (acc_ref[:d, :] / acc_ref[d:d + 1, :]).T
        o_ref[0, rows, :] = (o * _silu(g_ref[0, rows, :].astype(F32))).astype(o_ref.dtype)

    stage_a_diag(0, 0)

    def step(g, pair):
        iq, t = pair
        ends_tile = t == iq
        iq_next = jnp.where(ends_tile, iq + 1, iq)
        t_next = jnp.where(ends_tile, 0, t + 1)
        for slot in range(2):
            @pl.when((g % 2 == slot) & ends_tile)
            def _():
                stage_a_diag(1 - slot, iq_next)
                stage_bc(slot, iq, t)

            @pl.when((g % 2 == slot) & jnp.logical_not(ends_tile))
            def _():
                stage_a(1 - slot, iq_next, t_next)
                stage_bc(slot, iq, t)

        @pl.when(ends_tile)
        def _():
            finalize(iq)

        return iq_next, t_next

    iq, t = lax.fori_loop(0, n_pairs - 1, step, (jnp.int32(0), jnp.int32(0)))
    stage_bc((n_pairs - 1) % 2, iq, t)
    finalize(iq)


def fox_attention(qkvg, caug, *, tk=1024):
    b, s, _ = qkvg.shape
    hh, d = FOX_HEADS, FOX_HEAD_DIM
    assert s % tk == 0
    head_cols = lambda group: pl.BlockSpec((1, s, d), lambda bi, h: (bi, 0, group * hh + h))
    return pl.pallas_call(
        functools.partial(_attn_kernel, tk=tk),
        grid=(b, hh),
        in_specs=[head_cols(0), head_cols(1), pl.BlockSpec((1, s, LANES), lambda bi, h: (bi, 0, 0)),
                  head_cols(2), head_cols(3)],
        out_specs=head_cols(0),
        out_shape=jax.ShapeDtypeStruct((b, s, hh * d), BF16),
        scratch_shapes=[
            pltpu.VMEM((s, d + LANES), BF16),
            pltpu.VMEM((d + STRIP, s), BF16),
            pltpu.VMEM((s, d + LANES), BF16),
            pltpu.VMEM((tk, tk), F32), pltpu.VMEM((tk, tk), F32),
            pltpu.VMEM((1, tk), F32), pltpu.VMEM((1, tk), F32),
            pltpu.VMEM((d + STRIP, tk), F32),
            pltpu.VMEM((1, tk), F32),
        ],
        compiler_params=_params("parallel", "arbitrary"),
        name="fox_attention",
    )(qkvg, qkvg, caug, qkvg, qkvg)


def _conv_kernel(a_ref, b_ref, g_ref, ah_ref, bh_ref, dw_ref, dwb_ref, lng_ref, lnb_ref, o_ref,
                 u_ref, y_ref, *, rows):
    ts, c = o_ref.shape[1], o_ref.shape[2]
    first = pl.program_id(1) == 0

    def glu(a, b):
        return a.astype(F32) * _sigmoid(b.astype(F32))

    n_chunks = c // LANES

    def put_u(row, value):
        for ci in range(n_chunks):
            u_ref[ci, pl.ds(row, value.shape[0]), :] = value[:, ci * LANES:(ci + 1) * LANES]

    put_u(0, jnp.where(first, 0.0, glu(ah_ref[0], bh_ref[0])))

    def glu_strip(i, carry):
        r = pl.multiple_of(i * STRIP, STRIP)
        put_u(CONV_HALO + r, glu(a_ref[0, pl.ds(r, STRIP), :], b_ref[0, pl.ds(r, STRIP), :]))
        return carry

    lax.fori_loop(0, ts // STRIP, glu_strip, 0)

    shift = CONV_HALO - (CONV_KERNEL - 1)

    def chunk(ci, carry):
        lanes = pl.ds(pl.multiple_of(ci * LANES, LANES), LANES)
        taps = dw_ref[:, lanes]
        for r0 in range(0, ts, rows):
            u = [u_ref[ci, pl.ds(r0 + shift + m, SUBLANES, stride=SUBLANES), :]
                 for m in range(rows // SUBLANES + CONV_KERNEL - 1)]
            for j in range(rows // SUBLANES):
                acc = taps[0:1, :] * u[j]
                for kk in range(1, CONV_KERNEL):
                    acc = acc + taps[kk:kk + 1, :] * u[j + kk]
                y_ref[ci, pl.ds(r0 + j, SUBLANES, stride=SUBLANES), :] = acc
        return carry

    lax.fori_loop(0, n_chunks, chunk, 0)

    def norm_strip(i, carry):
        r = pl.multiple_of(i * STRIP, STRIP)
        y = jnp.concatenate([y_ref[ci, pl.ds(r, STRIP), :] for ci in range(n_chunks)], axis=1) + dwb_ref[...]
        mu = jnp.mean(y, axis=-1, keepdims=True)
        yc = y - mu
        var = jnp.mean(yc * yc, axis=-1, keepdims=True)
        z = yc * lax.rsqrt(var + LN_EPS) * lng_ref[...] + lnb_ref[...]
        gate = g_ref[0, pl.ds(r, STRIP), :].astype(F32)
        o_ref[0, pl.ds(r, STRIP), :] = (_silu(z) * _silu(gate)).astype(o_ref.dtype)
        return carry

    lax.fori_loop(0, ts // STRIP, norm_strip, 0, unroll=4)


def conv_module(abg, dw, dw_b, ln_g, ln_b, *, ts=256):
    b, s, c3 = abg.shape
    c = c3 // 3
    per = ts // CONV_HALO
    rows = SUBLANES * SUBLANES
    assert s % ts == 0 and ts % CONV_HALO == 0 and ts % rows == 0 and c % LANES == 0
    tile = lambda col: pl.BlockSpec((1, ts, c), lambda bi, i: (bi, i, col))
    halo = lambda col: pl.BlockSpec((1, CONV_HALO, c), lambda bi, i: (bi, jnp.maximum(i * per - 1, 0), col))
    vec = pl.BlockSpec((1, c), lambda bi, i: (0, 0))
    return pl.pallas_call(
        functools.partial(_conv_kernel, rows=rows),
        grid=(b, s // ts),
        in_specs=[tile(0), tile(1), tile(2), halo(0), halo(1),
                  pl.BlockSpec((CONV_KERNEL + 1, c), lambda bi, i: (0, 0)), vec, vec, vec],
        out_specs=pl.BlockSpec((1, ts, c), lambda bi, i: (bi, i, 0)),
        out_shape=jax.ShapeDtypeStruct((b, s, c), BF16),
        scratch_shapes=[pltpu.VMEM((c // LANES, ts + CONV_HALO, LANES), F32),
                        pltpu.VMEM((c // LANES, ts, LANES), F32)],
        compiler_params=_params("parallel", "arbitrary"),
        name="conv_module",
    )(abg, abg, abg, abg, abg, dw, dw_b, ln_g, ln_b)


def _out_kernel(y_ref, w_ref, h_ref, g_ref, *out_refs, last):
    h_new = h_ref[...] + jnp.dot(y_ref[...], w_ref[...], preferred_element_type=F32)
    normed = h_new * lax.rsqrt(jnp.mean(h_new * h_new, axis=-1, keepdims=True) + RMS_EPS) * g_ref[...]
    if last:
        out_refs[0][...] = normed
    else:
        out_refs[0][...] = h_new
        out_refs[1][...] = normed.astype(BF16)


def out_projection(y, w, layer, h, g, *, last, tm=512):
    m, k = y.shape
    d = w.shape[1]
    assert m % tm == 0 and w.shape[0] % k == 0 and layer < w.shape[0] // k
    row = lambda width: pl.BlockSpec((tm, width), lambda i: (i, 0))
    if last:
        out_specs, out_shape = row(d), jax.ShapeDtypeStruct((m, d), F32)
    else:
        out_specs = [row(d), row(d)]
        out_shape = [jax.ShapeDtypeStruct((m, d), F32), jax.ShapeDtypeStruct((m, d), BF16)]
    return pl.pallas_call(
        functools.partial(_out_kernel, last=last),
        grid=(m // tm,),
        in_specs=[row(k), pl.BlockSpec((k, d), lambda i: (layer, 0)), row(d),
                  pl.BlockSpec((1, d), lambda i: (0, 0))],
        out_specs=out_specs,
        out_shape=out_shape,
        compiler_params=_params("parallel"),
        name="out_proj",
    )(y, w, h, g)


def _forget_weights(w_f, b_f):
    lane = jnp.arange(LANES)
    spread = ((lane[None, :] // HEAD_LANES == jnp.arange(FOX_HEADS)[:, None])
              & (lane[None, :] % HEAD_LANES < 3)).astype(F32)
    wf = jnp.dot(w_f, spread, precision=lax.Precision.HIGHEST).astype(BF16)
    bf = jnp.dot(b_f[None, :], spread, precision=lax.Precision.HIGHEST)
    return wf, bf


def kernel(x, norm_g, fox_w_in, fox_b_f, fox_w_out, conv_w_in, conv_b_in, conv_dw, conv_dw_b, conv_ln_g,
           conv_ln_b, conv_w_out, final_norm_g):
    b, s, d = x.shape
    m = b * s
    h = x.reshape(m, d)
    hn = rmsnorm_bf16(h, norm_g[0][None, :])
    fox_w = fox_w_in[:, :, :4 * FOX_WIDTH].astype(BF16).reshape(-1, 4 * FOX_WIDTH)
    fox_wf = fox_w_in[:, :, 4 * FOX_WIDTH:]
    conv_w = conv_w_in.reshape(-1, conv_w_in.shape[-1])
    fox_w_o = fox_w_out.astype(BF16).reshape(-1, d)
    conv_w_o = conv_w_out.astype(BF16).reshape(-1, d)
    for i in range(DEPTH):
        j = i // 2
        if i % 2 == 0:
            wf, bf = _forget_weights(fox_wf[j], fox_b_f[j])
            qkvg = projection(hn, fox_w, j, 4 * FOX_WIDTH, jnp.zeros((1, 4 * FOX_WIDTH), F32),
                              scaled_cols=FOX_WIDTH, scale=FOX_HEAD_DIM ** -0.5 * LOG2E)
            caug = forget_gate(hn.reshape(b, s, d), wf, bf)
            y = fox_attention(qkvg.reshape(b, s, -1), caug).reshape(m, FOX_WIDTH)
            w_out = fox_w_o
        else:
            abg = projection(hn, conv_w, j, 3 * CONV_CHANNELS, conv_b_in[j][None, :]).reshape(b, s, -1)
            dw = jnp.concatenate([conv_dw[j], jnp.zeros((1, CONV_CHANNELS), F32)], axis=0)
            y = conv_module(abg, dw, conv_dw_b[j][None, :], conv_ln_g[j][None, :],
                            conv_ln_b[j][None, :]).reshape(m, CONV_CHANNELS)
            w_out = conv_w_o
        if i + 1 < DEPTH:
            h, hn = out_projection(y, w_out, j, h, norm_g[i + 1][None, :], last=False)
        else:
            out = out_projection(y, w_out, j, h, final_norm_g[None, :], last=True)
    return out.reshape(b, s, d)
```

```python
import functools
import math

import jax
import jax.numpy as jnp
from jax import lax
from jax.experimental import pallas as pl
from jax.experimental.pallas import tpu as pltpu

D_MODEL = 2048
DEPTH = 4
FOX_HEADS = 16
FOX_HEAD_DIM = 128
FOX_WIDTH = FOX_HEADS * FOX_HEAD_DIM
CONV_CHANNELS = D_MODEL
CONV_KERNEL = 31
RMS_EPS = 1e-6
LN_EPS = 1e-5

LANES = 128
SUBLANES = 8
LOG2E = 1.4426950408889634
VMEM_LIMIT = 56 * 1024 * 1024
HEAD_LANES = LANES // FOX_HEADS
MASK_VALUE = -1e30
CONV_HALO = 32
STRIP = 16

BF16 = jnp.bfloat16
F32 = jnp.float32


def _params(*sem):
    return pltpu.CompilerParams(dimension_semantics=sem, vmem_limit_bytes=VMEM_LIMIT)


def _split3(x):
    hi = x.astype(BF16).astype(F32)
    r = x - hi
    mid = r.astype(BF16).astype(F32)
    lo = (r - mid).astype(BF16).astype(F32)
    return hi, mid, lo


def _sigmoid(x):
    return 0.5 * jnp.tanh(0.5 * x) + 0.5


def _silu(x):
    half = 0.5 * x
    return half * (jnp.tanh(half) + 1.0)


def _proj_kernel(*refs, scaled_tiles, scale, normalise):
    if normalise:
        x_ref, g_ref, w_ref, b_ref, o_ref, hn_ref = refs

        @pl.when(pl.program_id(1) == 0)
        def _():
            x = x_ref[...]
            hn = x * lax.rsqrt(jnp.mean(x * x, axis=-1, keepdims=True) + RMS_EPS) * g_ref[...]
            hn_ref[...] = hn.astype(hn_ref.dtype)

        lhs = hn_ref[...]
    else:
        x_ref, w_ref, b_ref, o_ref = refs
        lhs = x_ref[...]
    acc = jnp.dot(lhs, w_ref[...].astype(BF16), preferred_element_type=F32) + b_ref[...]
    if scaled_tiles:
        acc = acc * jnp.where(pl.program_id(1) < scaled_tiles, scale, 1.0)
    o_ref[...] = acc.astype(o_ref.dtype)


def projection(x, w, layer, n_out, bias, *, norm_gain=None, scaled_cols=0, scale=1.0, tm=1024, tn=1024):
    m, k = x.shape
    assert m % tm == 0 and n_out % tn == 0 and scaled_cols % tn == 0 and n_out <= w.shape[1]
    assert w.shape[0] % k == 0 and layer < w.shape[0] // k
    normalise = norm_gain is not None
    rows = pl.BlockSpec((tm, k), lambda i, j: (i, 0))
    in_specs = [rows, pl.BlockSpec((k, tn), lambda i, j: (layer, j)), pl.BlockSpec((1, tn), lambda i, j: (0, j))]
    out_specs = pl.BlockSpec((tm, tn), lambda i, j: (i, j))
    out_shape = jax.ShapeDtypeStruct((m, n_out), BF16)
    operands = (x, w, bias)
    if normalise:
        in_specs.insert(1, pl.BlockSpec((1, k), lambda i, j: (0, 0)))
        out_specs, out_shape = [out_specs, rows], [out_shape, jax.ShapeDtypeStruct((m, k), BF16)]
        operands = (x, norm_gain, w, bias)
    return pl.pallas_call(
        functools.partial(_proj_kernel, scaled_tiles=scaled_cols // tn, scale=scale, normalise=normalise),
        grid=(m // tm, n_out // tn),
        in_specs=in_specs,
        out_specs=out_specs,
        out_shape=out_shape,
        compiler_params=_params("parallel", "arbitrary"),
        name="in_proj",
    )(*operands)


def _fgate_kernel(x_ref, wf_ref, bf_ref, o_ref, carry_ref):
    t = x_ref.shape[1]

    @pl.when(pl.program_id(1) == 0)
    def _():
        carry_ref[...] = jnp.zeros_like(carry_ref)

    z = jnp.dot(x_ref[0], wf_ref[...], preferred_element_type=F32) + bf_ref[...]
    log_f = jnp.minimum(z, 0.0) - jnp.log(1.0 + jnp.exp(-jnp.abs(z)))
    row = lax.broadcasted_iota(jnp.int32, (t, t), 0)
    col = lax.broadcasted_iota(jnp.int32, (t, t), 1)
    tri = (col <= row).astype(BF16)
    pieces = jnp.concatenate(_split3(log_f), axis=1).astype(BF16)
    local = jnp.dot(tri, pieces, preferred_element_type=F32)
    c = carry_ref[...] + local[:, :LANES] + local[:, LANES:2 * LANES] + local[:, 2 * LANES:]
    carry_ref[...] = c[t - 1:t, :]
    hi, mid, lo = _split3(c * (-LOG2E))
    piece = lax.broadcasted_iota(jnp.int32, (t, LANES), 1) % HEAD_LANES
    packed = jnp.where(piece == 0, hi, jnp.where(piece == 1, mid, jnp.where(piece == 2, lo, 0.0)))
    o_ref[0] = packed.astype(o_ref.dtype)


def forget_gate(hn, wf, bf, *, t=512):
    b, s, k = hn.shape
    assert s % t == 0
    return pl.pallas_call(
        _fgate_kernel,
        grid=(b, s // t),
        in_specs=[
            pl.BlockSpec((1, t, k), lambda bi, i: (bi, i, 0)),
            pl.BlockSpec((k, LANES), lambda bi, i: (0, 0)),
            pl.BlockSpec((1, LANES), lambda bi, i: (0, 0)),
        ],
        out_specs=pl.BlockSpec((1, t, LANES), lambda bi, i: (bi, i, 0)),
        out_shape=jax.ShapeDtypeStruct((b, s, LANES), BF16),
        scratch_shapes=[pltpu.VMEM((1, LANES), F32)],
        compiler_params=_params("parallel", "arbitrary"),
        name="forget_gate",
    )(hn, wf, bf)


def _attn_kernel(q_ref, k_ref, c_ref, v_ref, g_ref, o_ref,
                 kaug_ref, vaug_ref, qaug_ref, s0_ref, s1_ref, smax0_ref, smax1_ref, acc_ref, m_ref, *, tk):
    h = pl.program_id(1)
    s_len, d = q_ref.shape[1], q_ref.shape[2]
    tq = tk
    hk = tk // 2
    nq = s_len // tq
    n_pairs = nq * (nq + 1) // 2
    nt = (((1,), (1,)), ((), ()))

    kaug_ref[:, :d] = k_ref[0]
    kaug_ref[:, d:] = c_ref[0]
    vaug_ref[:d, :] = v_ref[0].astype(F32).T.astype(BF16)
    vaug_ref[d:, :] = jnp.ones((vaug_ref.shape[0] - d, s_len), BF16)
    col = lax.broadcasted_iota(jnp.int32, (s_len, LANES), 1)
    qaug_ref[:, :d] = q_ref[0]
    qaug_ref[:, d:] = ((col >= h * HEAD_LANES) & (col < h * HEAD_LANES + 3)).astype(BF16)
    acc_ref[...] = jnp.zeros_like(acc_ref)
    s_ref, smax_ref = (s0_ref, s1_ref), (smax0_ref, smax1_ref)

    def causal(s):
        q_pos = lax.broadcasted_iota(jnp.int32, s.shape, 1)
        k_pos = lax.broadcasted_iota(jnp.int32, s.shape, 0)
        return jnp.where(k_pos <= q_pos, s, MASK_VALUE)

    def scores(k_start, k_rows, q_start, q_rows):
        return lax.dot_general(kaug_ref[pl.ds(k_start, k_rows), :], qaug_ref[pl.ds(q_start, q_rows), :], nt,
                               preferred_element_type=F32)

    def stage_a(slot, iq, t):
        s = scores(pl.multiple_of((t - 1) * tk, tk), tk, pl.multiple_of(iq * tq, tq), tq)
        s_ref[slot][...] = s
        smax_ref[slot][...] = jnp.max(s, axis=0, keepdims=True)

    def stage_a_diag(slot, iq):
        base = pl.multiple_of(iq * tq, tq)
        top = scores(base, hk, base, tq)
        top_left = causal(top[:, :hk])
        bottom = causal(scores(base + hk, hk, base + hk, hk))
        s_ref[slot][:hk, :hk] = top_left
        s_ref[slot][:hk, hk:] = top[:, hk:]
        s_ref[slot][hk:, :hk] = jnp.full((hk, hk), MASK_VALUE, F32)
        s_ref[slot][hk:, hk:] = bottom
        smax_ref[slot][:, :hk] = jnp.max(top_left, axis=0, keepdims=True)
        smax_ref[slot][:, hk:] = jnp.maximum(jnp.max(top[:, hk:], axis=0, keepdims=True),
                                             jnp.max(bottom, axis=0, keepdims=True))

    def stage_bc(slot, iq, t):
        k_start = pl.multiple_of(jnp.where(t == 0, iq * tq, (t - 1) * tk), tk)
        m_old = jnp.where(t == 0, MASK_VALUE, m_ref[...])
        m_new = jnp.maximum(m_old, smax_ref[slot][...])
        m_ref[...] = m_new
        p = jnp.exp2((s_ref[slot][...] - m_new).astype(BF16))
        pv = jnp.dot(vaug_ref[:, pl.ds(k_start, tk)], p, preferred_element_type=F32)
        acc_ref[...] = jnp.exp2(m_old - m_new) * acc_ref[...] + pv

    def finalize(iq):
        rows = pl.ds(pl.multiple_of(iq * tq, tq), tq)
        o = (acc_ref[:d, :] / acc_ref[d:d + 1, :]).T
        o_ref[0, rows, :] = (o * _silu(g_ref[0, rows, :].astype(F32))).astype(o_ref.dtype)

    stage_a_diag(0, 0)

    def step(g, pair):
        iq, t = pair
        ends_tile = t == iq
        iq_next = jnp.where(ends_tile, iq + 1, iq)
        t_next = jnp.where(ends_tile, 0, t + 1)
        for slot in range(2):
            @pl.when((g % 2 == slot) & ends_tile)
            def _():
                stage_a_diag(1 - slot, iq_next)
                stage_bc(slot, iq, t)

            @pl.when((g % 2 == slot) & jnp.logical_not(ends_tile))
            def _():
                stage_a(1 - slot, iq_next, t_next)
                stage_bc(slot, iq, t)

        @pl.when(ends_tile)
        def _():
            finalize(iq)

        return iq_next, t_next

    iq, t = lax.fori_loop(0, n_pairs - 1, step, (jnp.int32(0), jnp.int32(0)))
    stage_bc((n_pairs - 1) % 2, iq, t)
    finalize(iq)


def fox_attention(qkvg, caug, *, tk=1024):
    b, s, _ = qkvg.shape
    hh, d = FOX_HEADS, FOX_HEAD_DIM
    assert s % tk == 0
    head_cols = lambda group: pl.BlockSpec((1, s, d), lambda bi, h: (bi, 0, group * hh + h))
    return pl.pallas_call(
        functools.partial(_attn_kernel, tk=tk),
        grid=(b, hh),
        in_specs=[head_cols(0), head_cols(1), pl.BlockSpec((1, s, LANES), lambda bi, h: (bi, 0, 0)),
                  head_cols(2), head_cols(3)],
        out_specs=head_cols(0),
        out_shape=jax.ShapeDtypeStruct((b, s, hh * d), BF16),
        scratch_shapes=[
            pltpu.VMEM((s, d + LANES), BF16),
            pltpu.VMEM((d + STRIP, s), BF16),
            pltpu.VMEM((s, d + LANES), BF16),
            pltpu.VMEM((tk, tk), F32), pltpu.VMEM((tk, tk), F32),
            pltpu.VMEM((1, tk), F32), pltpu.VMEM((1, tk), F32),
            pltpu.VMEM((d + STRIP, tk), F32),
            pltpu.VMEM((1, tk), F32),
        ],
        compiler_params=_params("parallel", "arbitrary"),
        name="fox_attention",
    )(qkvg, qkvg, caug, qkvg, qkvg)


def _conv_kernel(a_ref, b_ref, g_ref, ah_ref, bh_ref, dw_ref, dwb_ref, lng_ref, lnb_ref, o_ref,
                 u_ref, y_ref, *, rows):
    ts, c = o_ref.shape[1], o_ref.shape[2]
    first = pl.program_id(1) == 0

    def glu(a, b):
        return a.astype(F32) * _sigmoid(b.astype(F32))

    n_chunks = c // LANES

    def put_u(row, value):
        for ci in range(n_chunks):
            u_ref[ci, pl.ds(row, value.shape[0]), :] = value[:, ci * LANES:(ci + 1) * LANES]

    put_u(0, jnp.where(first, 0.0, glu(ah_ref[0], bh_ref[0])))

    def glu_strip(i, carry):
        r = pl.multiple_of(i * STRIP, STRIP)
        put_u(CONV_HALO + r, glu(a_ref[0, pl.ds(r, STRIP), :], b_ref[0, pl.ds(r, STRIP), :]))
        return carry

    lax.fori_loop(0, ts // STRIP, glu_strip, 0)

    shift = CONV_HALO - (CONV_KERNEL - 1)

    def chunk(ci, carry):
        lanes = pl.ds(pl.multiple_of(ci * LANES, LANES), LANES)
        taps = dw_ref[:, lanes]
        for r0 in range(0, ts, rows):
            u = [u_ref[ci, pl.ds(r0 + shift + m, SUBLANES, stride=SUBLANES), :]
                 for m in range(rows // SUBLANES + CONV_KERNEL - 1)]
            for j in range(rows // SUBLANES):
                acc = taps[0:1, :] * u[j]
                for kk in range(1, CONV_KERNEL):
                    acc = acc + taps[kk:kk + 1, :] * u[j + kk]
                y_ref[ci, pl.ds(r0 + j, SUBLANES, stride=SUBLANES), :] = acc
        return carry

    lax.fori_loop(0, n_chunks, chunk, 0)

    def norm_strip(i, carry):
        r = pl.multiple_of(i * STRIP, STRIP)
        y = jnp.concatenate([y_ref[ci, pl.ds(r, STRIP), :] for ci in range(n_chunks)], axis=1) + dwb_ref[...]
        mu = jnp.mean(y, axis=-1, keepdims=True)
        yc = y - mu
        var = jnp.mean(yc * yc, axis=-1, keepdims=True)
        z = yc * lax.rsqrt(var + LN_EPS) * lng_ref[...] + lnb_ref[...]
        gate = g_ref[0, pl.ds(r, STRIP), :].astype(F32)
        o_ref[0, pl.ds(r, STRIP), :] = (_silu(z) * _silu(gate)).astype(o_ref.dtype)
        return carry

    lax.fori_loop(0, ts // STRIP, norm_strip, 0, unroll=4)


def conv_module(abg, dw, dw_b, ln_g, ln_b, *, ts=256):
    b, s, c3 = abg.shape
    c = c3 // 3
    per = ts // CONV_HALO
    rows = SUBLANES * SUBLANES
    assert s % ts == 0 and ts % CONV_HALO == 0 and ts % rows == 0 and c % LANES == 0
    tile = lambda col: pl.BlockSpec((1, ts, c), lambda bi, i: (bi, i, col))
    halo = lambda col: pl.BlockSpec((1, CONV_HALO, c), lambda bi, i: (bi, jnp.maximum(i * per - 1, 0), col))
    vec = pl.BlockSpec((1, c), lambda bi, i: (0, 0))
    return pl.pallas_call(
        functools.partial(_conv_kernel, rows=rows),
        grid=(b, s // ts),
        in_specs=[tile(0), tile(1), tile(2), halo(0), halo(1),
                  pl.BlockSpec((CONV_KERNEL + 1, c), lambda bi, i: (0, 0)), vec, vec, vec],
        out_specs=pl.BlockSpec((1, ts, c), lambda bi, i: (bi, i, 0)),
        out_shape=jax.ShapeDtypeStruct((b, s, c), BF16),
        scratch_shapes=[pltpu.VMEM((c // LANES, ts + CONV_HALO, LANES), F32),
                        pltpu.VMEM((c // LANES, ts, LANES), F32)],
        compiler_params=_params("parallel", "arbitrary"),
        name="conv_module",
    )(abg, abg, abg, abg, abg, dw, dw_b, ln_g, ln_b)


def _out_kernel(y_ref, w_ref, h_ref, g_ref, *out_refs, last):
    h_new = h_ref[...] + jnp.dot(y_ref[...], w_ref[...], preferred_element_type=F32)
    normed = h_new * lax.rsqrt(jnp.mean(h_new * h_new, axis=-1, keepdims=True) + RMS_EPS) * g_ref[...]
    if last:
        out_refs[0][...] = normed
    else:
        out_refs[0][...] = h_new
        out_refs[1][...] = normed.astype(BF16)


def out_projection(y, w, layer, h, g, *, last, tm=512):
    m, k = y.shape
    d = w.shape[1]
    assert m % tm == 0 and w.shape[0] % k == 0 and layer < w.shape[0] // k
    row = lambda width: pl.BlockSpec((tm, width), lambda i: (i, 0))
    if last:
        out_specs, out_shape = row(d), jax.ShapeDtypeStruct((m, d), F32)
    else:
        out_specs = [row(d), row(d)]
        out_shape = [jax.ShapeDtypeStruct((m, d), F32), jax.ShapeDtypeStruct((m, d), BF16)]
    return pl.pallas_call(
        functools.partial(_out_kernel, last=last),
        grid=(m // tm,),
        in_specs=[row(k), pl.BlockSpec((k, d), lambda i: (layer, 0)), row(d),
                  pl.BlockSpec((1, d), lambda i: (0, 0))],
        out_specs=out_specs,
        out_shape=out_shape,
        compiler_params=_params("parallel"),
        name="out_proj",
    )(y, w, h, g)


def _forget_weights(w_f, b_f):
    lane = jnp.arange(LANES)
    spread = ((lane[None, :] // HEAD_LANES == jnp.arange(FOX_HEADS)[:, None])
              & (lane[None, :] % HEAD_LANES < 3)).astype(F32)
    wf = jnp.dot(w_f, spread, precision=lax.Precision.HIGHEST).astype(BF16)
    bf = jnp.dot(b_f[None, :], spread, precision=lax.Precision.HIGHEST)
    return wf, bf


def kernel(x, norm_g, fox_w_in, fox_b_f, fox_w_out, conv_w_in, conv_b_in, conv_dw, conv_dw_b, conv_ln_g,
           conv_ln_b, conv_w_out, final_norm_g):
    b, s, d = x.shape
    m = b * s
    h = x.reshape(m, d)
    fox_w = fox_w_in[:, :, :4 * FOX_WIDTH].astype(BF16).reshape(-1, 4 * FOX_WIDTH)
    fox_wf = fox_w_in[:, :, 4 * FOX_WIDTH:]
    conv_w = conv_w_in.reshape(-1, conv_w_in.shape[-1])
    fox_w_o = fox_w_out.astype(BF16).reshape(-1, d)
    conv_w_o = conv_w_out.astype(BF16).reshape(-1, d)
    for i in range(DEPTH):
        j = i // 2
        if i % 2 == 0:
            wf, bf = _forget_weights(fox_wf[j], fox_b_f[j])
            q_scale = dict(scaled_cols=FOX_WIDTH, scale=FOX_HEAD_DIM ** -0.5 * LOG2E)
            no_bias = jnp.zeros((1, 4 * FOX_WIDTH), F32)
            if i == 0:
                qkvg, hn = projection(h, fox_w, j, 4 * FOX_WIDTH, no_bias, norm_gain=norm_g[0][None, :], **q_scale)
            else:
                qkvg = projection(hn, fox_w, j, 4 * FOX_WIDTH, no_bias, tn=2048, **q_scale)
            caug = forget_gate(hn.reshape(b, s, d), wf, bf)
            y = fox_attention(qkvg.reshape(b, s, -1), caug).reshape(m, FOX_WIDTH)
            w_out = fox_w_o
        else:
            abg = projection(hn, conv_w, j, 3 * CONV_CHANNELS, conv_b_in[j][None, :]).reshape(b, s, -1)
            dw = jnp.concatenate([conv_dw[j], jnp.zeros((1, CONV_CHANNELS), F32)], axis=0)
            y = conv_module(abg, dw, conv_dw_b[j][None, :], conv_ln_g[j][None, :],
                            conv_ln_b[j][None, :]).reshape(m, CONV_CHANNELS)
            w_out = conv_w_o
        if i + 1 < DEPTH:
            h, hn = out_projection(y, w_out, j, h, norm_g[i + 1][None, :], last=False)
        else:
            out = out_projection(y, w_out, j, h, final_norm_g[None, :], last=True)
    return out.reshape(b, s, d)
```

```python
import functools
import math

import jax
import jax.numpy as jnp
from jax import lax
from jax.experimental import pallas as pl
from jax.experimental.pallas import tpu as pltpu

D_MODEL = 2048
DEPTH = 4
FOX_HEADS = 16
FOX_HEAD_DIM = 128
FOX_WIDTH = FOX_HEADS * FOX_HEAD_DIM
CONV_CHANNELS = D_MODEL
CONV_KERNEL = 31
RMS_EPS = 1e-6
LN_EPS = 1e-5

LANES = 128
SUBLANES = 8
LOG2E = 1.4426950408889634
VMEM_LIMIT = 56 * 1024 * 1024
HEAD_LANES = LANES // FOX_HEADS
MASK_VALUE = -1e30
CONV_HALO = 32
STRIP = 16

BF16 = jnp.bfloat16
F32 = jnp.float32


def _params(*sem):
    return pltpu.CompilerParams(dimension_semantics=sem, vmem_limit_bytes=VMEM_LIMIT)


def _split3(x):
    hi = x.astype(BF16).astype(F32)
    r = x - hi
    mid = r.astype(BF16).astype(F32)
    lo = (r - mid).astype(BF16).astype(F32)
    return hi, mid, lo


def _sigmoid(x):
    return 0.5 * jnp.tanh(0.5 * x) + 0.5


def _silu(x):
    half = 0.5 * x
    return half * (jnp.tanh(half) + 1.0)


def _proj_kernel(*refs, scaled_tiles, scale, normalise):
    if normalise:
        x_ref, g_ref, w_ref, b_ref, o_ref, hn_ref = refs

        @pl.when(pl.program_id(1) == 0)
        def _():
            x = x_ref[...]
            hn = x * lax.rsqrt(jnp.mean(x * x, axis=-1, keepdims=True) + RMS_EPS) * g_ref[...]
            hn_ref[...] = hn.astype(hn_ref.dtype)

        lhs = hn_ref[...]
    else:
        x_ref, w_ref, b_ref, o_ref = refs
        lhs = x_ref[...]
    acc = jnp.dot(lhs, w_ref[...].astype(BF16), preferred_element_type=F32) + b_ref[...]
    if scaled_tiles:
        acc = acc * jnp.where(pl.program_id(1) < scaled_tiles, scale, 1.0)
    o_ref[...] = acc.astype(o_ref.dtype)


def projection(x, w, layer, n_out, bias, *, norm_gain=None, scaled_cols=0, scale=1.0, tm=1024, tn=1024):
    m, k = x.shape
    assert m % tm == 0 and n_out % tn == 0 and scaled_cols % tn == 0 and n_out <= w.shape[1]
    assert w.shape[0] % k == 0 and layer < w.shape[0] // k
    normalise = norm_gain is not None
    rows = pl.BlockSpec((tm, k), lambda i, j: (i, 0))
    in_specs = [rows, pl.BlockSpec((k, tn), lambda i, j: (layer, j)), pl.BlockSpec((1, tn), lambda i, j: (0, j))]
    out_specs = pl.BlockSpec((tm, tn), lambda i, j: (i, j))
    out_shape = jax.ShapeDtypeStruct((m, n_out), BF16)
    operands = (x, w, bias)
    if normalise:
        in_specs.insert(1, pl.BlockSpec((1, k), lambda i, j: (0, 0)))
        out_specs, out_shape = [out_specs, rows], [out_shape, jax.ShapeDtypeStruct((m, k), BF16)]
        operands = (x, norm_gain, w, bias)
    return pl.pallas_call(
        functools.partial(_proj_kernel, scaled_tiles=scaled_cols // tn, scale=scale, normalise=normalise),
        grid=(m // tm, n_out // tn),
        in_specs=in_specs,
        out_specs=out_specs,
        out_shape=out_shape,
        compiler_params=_params("parallel", "arbitrary"),
        name="in_proj",
    )(*operands)


def _fgate_kernel(x_ref, wf_ref, bf_ref, o_ref, carry_ref):
    t = x_ref.shape[1]

    @pl.when(pl.program_id(1) == 0)
    def _():
        carry_ref[...] = jnp.zeros_like(carry_ref)

    z = jnp.dot(x_ref[0], wf_ref[...], preferred_element_type=F32) + bf_ref[...]
    log_f = jnp.minimum(z, 0.0) - jnp.log(1.0 + jnp.exp(-jnp.abs(z)))
    row = lax.broadcasted_iota(jnp.int32, (t, t), 0)
    col = lax.broadcasted_iota(jnp.int32, (t, t), 1)
    tri = (col <= row).astype(BF16)
    pieces = jnp.concatenate(_split3(log_f), axis=1).astype(BF16)
    local = jnp.dot(tri, pieces, preferred_element_type=F32)
    c = carry_ref[...] + local[:, :LANES] + local[:, LANES:2 * LANES] + local[:, 2 * LANES:]
    carry_ref[...] = c[t - 1:t, :]
    hi, mid, lo = _split3(c * (-LOG2E))
    piece = lax.broadcasted_iota(jnp.int32, (t, LANES), 1) % HEAD_LANES
    packed = jnp.where(piece == 0, hi, jnp.where(piece == 1, mid, jnp.where(piece == 2, lo, 0.0)))
    o_ref[0] = packed.astype(o_ref.dtype)


def forget_gate(hn, wf, bf, *, t=512):
    b, s, k = hn.shape
    assert s % t == 0
    return pl.pallas_call(
        _fgate_kernel,
        grid=(b, s // t),
        in_specs=[
            pl.BlockSpec((1, t, k), lambda bi, i: (bi, i, 0)),
            pl.BlockSpec((k, LANES), lambda bi, i: (0, 0)),
            pl.BlockSpec((1, LANES), lambda bi, i: (0, 0)),
        ],
        out_specs=pl.BlockSpec((1, t, LANES), lambda bi, i: (bi, i, 0)),
        out_shape=jax.ShapeDtypeStruct((b, s, LANES), BF16),
        scratch_shapes=[pltpu.VMEM((1, LANES), F32)],
        compiler_params=_params("parallel", "arbitrary"),
        name="forget_gate",
    )(hn, wf, bf)


def _attn_kernel(q_ref, k_ref, c_ref, v_ref, g_ref, o_ref,
                 kaug_ref, vaug_ref, qaug_ref, s0_ref, s1_ref, smax0_ref, smax1_ref, acc_ref, m_ref, *, tk):
    h = pl.program_id(1)
    s_len, d = q_ref.shape[1], q_ref.shape[2]
    tq = tk
    hk = tk // 2
    nq = s_len // tq
    n_pairs = nq * (nq + 1) // 2
    nt = (((1,), (1,)), ((), ()))

    kaug_ref[:, :d] = k_ref[0]
    kaug_ref[:, d:] = c_ref[0]
    vaug_ref[:d, :] = v_ref[0].astype(F32).T.astype(BF16)
    vaug_ref[d:, :] = jnp.ones((vaug_ref.shape[0] - d, s_len), BF16)
    col = lax.broadcasted_iota(jnp.int32, (s_len, LANES), 1)
    qaug_ref[:, :d] = q_ref[0]
    qaug_ref[:, d:] = ((col >= h * HEAD_LANES) & (col < h * HEAD_LANES + 3)).astype(BF16)
    s_ref, smax_ref = (s0_ref, s1_ref), (smax0_ref, smax1_ref)

    def causal(s):
        q_pos = lax.broadcasted_iota(jnp.int32, s.shape, 1)
        k_pos = lax.broadcasted_iota(jnp.int32, s.shape, 0)
        return jnp.where(k_pos <= q_pos, s, MASK_VALUE)

    def scores(k_start, k_rows, q_start, q_rows):
        return lax.dot_general(kaug_ref[pl.ds(k_start, k_rows), :], qaug_ref[pl.ds(q_start, q_rows), :], nt,
                               preferred_element_type=F32)

    def stage_a(slot, iq, t):
        s = scores(pl.multiple_of((t - 1) * tk, tk), tk, pl.multiple_of(iq * tq, tq), tq)
        s_ref[slot][...] = s
        smax_ref[slot][...] = jnp.max(s, axis=0, keepdims=True)

    def stage_a_diag(slot, iq):
        base = pl.multiple_of(iq * tq, tq)
        top = scores(base, hk, base, tq)
        top_left = causal(top[:, :hk])
        bottom = causal(scores(base + hk, hk, base + hk, hk))
        s_ref[slot][:hk, :hk] = top_left
        s_ref[slot][:hk, hk:] = top[:, hk:]
        s_ref[slot][hk:, hk:] = bottom
        smax_ref[slot][:, :hk] = jnp.max(top_left, axis=0, keepdims=True)
        smax_ref[slot][:, hk:] = jnp.maximum(jnp.max(top[:, hk:], axis=0, keepdims=True),
                                             jnp.max(bottom, axis=0, keepdims=True))

    def stage_bc(slot, iq, t):
        m_old = m_ref[...]
        m_new = jnp.maximum(m_old, smax_ref[slot][...])
        m_ref[...] = m_new
        p = jnp.exp2((s_ref[slot][...] - m_new).astype(BF16))
        pv = jnp.dot(vaug_ref[:, pl.ds(pl.multiple_of((t - 1) * tk, tk), tk)], p, preferred_element_type=F32)
        acc_ref[...] = jnp.exp2(m_old - m_new) * acc_ref[...] + pv

    def stage_bc_diag(slot, iq):
        base = pl.multiple_of(iq * tq, tq)
        m_new = smax_ref[slot][...]
        m_ref[...] = m_new
        p_left = jnp.exp2((s_ref[slot][:hk, :hk] - m_new[:, :hk]).astype(BF16))
        p_right = jnp.exp2((s_ref[slot][:, hk:] - m_new[:, hk:]).astype(BF16))
        acc_ref[:, :hk] = jnp.dot(vaug_ref[:, pl.ds(base, hk)], p_left, preferred_element_type=F32)
        acc_ref[:, hk:] = jnp.dot(vaug_ref[:, pl.ds(base, tk)], p_right, preferred_element_type=F32)

    def finalize(iq):
        rows = pl.ds(pl.multiple_of(iq * tq, tq), tq)
        o = (acc_ref[:d, :] / acc_ref[d:d + 1, :]).T
        o_ref[0, rows, :] = (o * _silu(g_ref[0, rows, :].astype(F32))).astype(o_ref.dtype)

    stage_a_diag(0, 0)
    if n_pairs > 1:
        stage_a_diag(1, 1)
    stage_bc_diag(0, 0)
    finalize(0)
    if n_pairs == 1:
        return

    def step(g, pair):
        iq, t = pair
        opens_tile = t == 0
        ends_tile = t == iq
        for slot in range(2):
            in_slot = g % 2 == slot

            @pl.when(in_slot & opens_tile)
            def _():
                stage_a(1 - slot, iq, 1)
                stage_bc_diag(slot, iq)

            @pl.when(in_slot & ends_tile)
            def _():
                stage_a_diag(1 - slot, iq + 1)
                stage_bc(slot, iq, t)

            @pl.when(in_slot & jnp.logical_not(opens_tile | ends_tile))
            def _():
                stage_a(1 - slot, iq, t + 1)
                stage_bc(slot, iq, t)

        @pl.when(ends_tile)
        def _():
            finalize(iq)

        return jnp.where(ends_tile, iq + 1, iq), jnp.where(ends_tile, 0, t + 1)

    iq, t = lax.fori_loop(1, n_pairs - 1, step, (jnp.int32(1), jnp.int32(0)))
    stage_bc((n_pairs - 1) % 2, iq, t)
    finalize(iq)


def fox_attention(qkvg, caug, *, tk=1024):
    b, s, _ = qkvg.shape
    hh, d = FOX_HEADS, FOX_HEAD_DIM
    assert s % tk == 0
    head_cols = lambda group: pl.BlockSpec((1, s, d), lambda bi, h: (bi, 0, group * hh + h))
    return pl.pallas_call(
        functools.partial(_attn_kernel, tk=tk),
        grid=(b, hh),
        in_specs=[head_cols(0), head_cols(1), pl.BlockSpec((1, s, LANES), lambda bi, h: (bi, 0, 0)),
                  head_cols(2), head_cols(3)],
        out_specs=head_cols(0),
        out_shape=jax.ShapeDtypeStruct((b, s, hh * d), BF16),
        scratch_shapes=[
            pltpu.VMEM((s, d + LANES), BF16),
            pltpu.VMEM((d + STRIP, s), BF16),
            pltpu.VMEM((s, d + LANES), BF16),
            pltpu.VMEM((tk, tk), F32), pltpu.VMEM((tk, tk), F32),
            pltpu.VMEM((1, tk), F32), pltpu.VMEM((1, tk), F32),
            pltpu.VMEM((d + STRIP, tk), F32),
            pltpu.VMEM((1, tk), F32),
        ],
        compiler_params=_params("parallel", "arbitrary"),
        name="fox_attention",
    )(qkvg, qkvg, caug, qkvg, qkvg)


def _conv_kernel(a_ref, b_ref, g_ref, ah_ref, bh_ref, dw_ref, dwb_ref, lng_ref, lnb_ref, o_ref,
                 u_ref, y_ref, *, rows):
    ts, c = o_ref.shape[1], o_ref.shape[2]
    first = pl.program_id(1) == 0

    def glu(a, b):
        return a.astype(F32) * _sigmoid(b.astype(F32))

    n_chunks = c // LANES

    def put_u(row, value):
        for ci in range(n_chunks):
            u_ref[ci, pl.ds(row, value.shape[0]), :] = value[:, ci * LANES:(ci + 1) * LANES]

    put_u(0, jnp.where(first, 0.0, glu(ah_ref[0], bh_ref[0])))

    def glu_strip(i, carry):
        r = pl.multiple_of(i * STRIP, STRIP)
        put_u(CONV_HALO + r, glu(a_ref[0, pl.ds(r, STRIP), :], b_ref[0, pl.ds(r, STRIP), :]))
        return carry

    lax.fori_loop(0, ts // STRIP, glu_strip, 0)

    shift = CONV_HALO - (CONV_KERNEL - 1)

    def chunk(ci, carry):
        lanes = pl.ds(pl.multiple_of(ci * LANES, LANES), LANES)
        taps = dw_ref[:, lanes]
        for r0 in range(0, ts, rows):
            u = [u_ref[ci, pl.ds(r0 + shift + m, SUBLANES, stride=SUBLANES), :]
                 for m in range(rows // SUBLANES + CONV_KERNEL - 1)]
            for j in range(rows // SUBLANES):
                acc = taps[0:1, :] * u[j]
                for kk in range(1, CONV_KERNEL):
                    acc = acc + taps[kk:kk + 1, :] * u[j + kk]
                y_ref[ci, pl.ds(r0 + j, SUBLANES, stride=SUBLANES), :] = acc
        return carry

    lax.fori_loop(0, n_chunks, chunk, 0)

    def norm_strip(i, carry):
        r = pl.multiple_of(i * STRIP, STRIP)
        y = jnp.concatenate([y_ref[ci, pl.ds(r, STRIP), :] for ci in range(n_chunks)], axis=1) + dwb_ref[...]
        mu = jnp.mean(y, axis=-1, keepdims=True)
        yc = y - mu
        var = jnp.mean(yc * yc, axis=-1, keepdims=True)
        z = yc * lax.rsqrt(var + LN_EPS) * lng_ref[...] + lnb_ref[...]
        gate = g_ref[0, pl.ds(r, STRIP), :].astype(F32)
        o_ref[0, pl.ds(r, STRIP), :] = (_silu(z) * _silu(gate)).astype(o_ref.dtype)
        return carry

    lax.fori_loop(0, ts // STRIP, norm_strip, 0, unroll=4)


def conv_module(abg, dw, dw_b, ln_g, ln_b, *, ts=256):
    b, s, c3 = abg.shape
    c = c3 // 3
    per = ts // CONV_HALO
    rows = SUBLANES * SUBLANES
    assert s % ts == 0 and ts % CONV_HALO == 0 and ts % rows == 0 and c % LANES == 0
    tile = lambda col: pl.BlockSpec((1, ts, c), lambda bi, i: (bi, i, col))
    halo = lambda col: pl.BlockSpec((1, CONV_HALO, c), lambda bi, i: (bi, jnp.maximum(i * per - 1, 0), col))
    vec = pl.BlockSpec((1, c), lambda bi, i: (0, 0))
    return pl.pallas_call(
        functools.partial(_conv_kernel, rows=rows),
        grid=(b, s // ts),
        in_specs=[tile(0), tile(1), tile(2), halo(0), halo(1),
                  pl.BlockSpec((CONV_KERNEL + 1, c), lambda bi, i: (0, 0)), vec, vec, vec],
        out_specs=pl.BlockSpec((1, ts, c), lambda bi, i: (bi, i, 0)),
        out_shape=jax.ShapeDtypeStruct((b, s, c), BF16),
        scratch_shapes=[pltpu.VMEM((c // LANES, ts + CONV_HALO, LANES), F32),
                        pltpu.VMEM((c // LANES, ts, LANES), F32)],
        compiler_params=_params("parallel", "arbitrary"),
        name="conv_module",
    )(abg, abg, abg, abg, abg, dw, dw_b, ln_g, ln_b)


def _out_kernel(y_ref, w_ref, h_ref, g_ref, *out_refs, last):
    h_new = h_ref[...] + jnp.dot(y_ref[...], w_ref[...], preferred_element_type=F32)
    normed = h_new * lax.rsqrt(jnp.mean(h_new * h_new, axis=-1, keepdims=True) + RMS_EPS) * g_ref[...]
    if last:
        out_refs[0][...] = normed
    else:
        out_refs[0][...] = h_new
        out_refs[1][...] = normed.astype(BF16)


def out_projection(y, w, layer, h, g, *, last, tm=512):
    m, k = y.shape
    d = w.shape[1]
    assert m % tm == 0 and w.shape[0] % k == 0 and layer < w.shape[0] // k
    row = lambda width: pl.BlockSpec((tm, width), lambda i: (i, 0))
    if last:
        out_specs, out_shape = row(d), jax.ShapeDtypeStruct((m, d), F32)
    else:
        out_specs = [row(d), row(d)]
        out_shape = [jax.ShapeDtypeStruct((m, d), F32), jax.ShapeDtypeStruct((m, d), BF16)]
    return pl.pallas_call(
        functools.partial(_out_kernel, last=last),
        grid=(m // tm,),
        in_specs=[row(k), pl.BlockSpec((k, d), lambda i: (layer, 0)), row(d),
                  pl.BlockSpec((1, d), lambda i: (0, 0))],
        out_specs=out_specs,
        out_shape=out_shape,
        compiler_params=_params("parallel"),
        name="out_proj",
    )(y, w, h, g)


def _forget_weights(w_f, b_f):
    lane = jnp.arange(LANES)
    spread = ((lane[None, :] // HEAD_LANES == jnp.arange(FOX_HEADS)[:, None])
              & (lane[None, :] % HEAD_LANES < 3)).astype(F32)
    wf = jnp.dot(w_f, spread, precision=lax.Precision.HIGHEST).astype(BF16)
    bf = jnp.dot(b_f[None, :], spread, precision=lax.Precision.HIGHEST)
    return wf, bf


def kernel(x, norm_g, fox_w_in, fox_b_f, fox_w_out, conv_w_in, conv_b_in, conv_dw, conv_dw_b, conv_ln_g,
           conv_ln_b, conv_w_out, final_norm_g):
    b, s, d = x.shape
    m = b * s
    h = x.reshape(m, d)
    fox_w = fox_w_in[:, :, :4 * FOX_WIDTH].astype(BF16).reshape(-1, 4 * FOX_WIDTH)
    fox_wf = fox_w_in[:, :, 4 * FOX_WIDTH:]
    conv_w = conv_w_in.reshape(-1, conv_w_in.shape[-1])
    fox_w_o = fox_w_out.astype(BF16).reshape(-1, d)
    conv_w_o = conv_w_out.astype(BF16).reshape(-1, d)
    for i in range(DEPTH):
        j = i // 2
        if i % 2 == 0:
            wf, bf = _forget_weights(fox_wf[j], fox_b_f[j])
            q_scale = dict(scaled_cols=FOX_WIDTH, scale=FOX_HEAD_DIM ** -0.5 * LOG2E)
            no_bias = jnp.zeros((1, 4 * FOX_WIDTH), F32)
            if i == 0:
                qkvg, hn = projection(h, fox_w, j, 4 * FOX_WIDTH, no_bias, norm_gain=norm_g[0][None, :], **q_scale)
            else:
                qkvg = projection(hn, fox_w, j, 4 * FOX_WIDTH, no_bias, tn=2048, **q_scale)
            caug = forget_gate(hn.reshape(b, s, d), wf, bf)
            y = fox_attention(qkvg.reshape(b, s, -1), caug).reshape(m, FOX_WIDTH)
            w_out = fox_w_o
        else:
            abg = projection(hn, conv_w, j, 3 * CONV_CHANNELS, conv_b_in[j][None, :]).reshape(b, s, -1)
            dw = jnp.concatenate([conv_dw[j], jnp.zeros((1, CONV_CHANNELS), F32)], axis=0)
            y = conv_module(abg, dw, conv_dw_b[j][None, :], conv_ln_g[j][None, :],
                            conv_ln_b[j][None, :]).reshape(m, CONV_CHANNELS)
            w_out = conv_w_o
        if i + 1 < DEPTH:
            h, hn = out_projection(y, w_out, j, h, norm_g[i + 1][None, :], last=False)
        else:
            out = out_projection(y, w_out, j, h, final_norm_g[None, :], last=True)
    return out.reshape(b, s, d)
```

```python
import functools
import math

import jax
import jax.numpy as jnp
from jax import lax
from jax.experimental import pallas as pl
from jax.experimental.pallas import tpu as pltpu

D_MODEL = 2048
DEPTH = 4
FOX_HEADS = 16
FOX_HEAD_DIM = 128
FOX_WIDTH = FOX_HEADS * FOX_HEAD_DIM
CONV_CHANNELS = D_MODEL
CONV_KERNEL = 31
RMS_EPS = 1e-6
LN_EPS = 1e-5

LANES = 128
SUBLANES = 8
LOG2E = 1.4426950408889634
VMEM_LIMIT = 56 * 1024 * 1024
HEAD_LANES = LANES // FOX_HEADS
MASK_VALUE = -1e30
CONV_HALO = 32
STRIP = 16

BF16 = jnp.bfloat16
F32 = jnp.float32


def _params(*sem):
    return pltpu.CompilerParams(dimension_semantics=sem, vmem_limit_bytes=VMEM_LIMIT)


def _split3(x):
    hi = x.astype(BF16).astype(F32)
    r = x - hi
    mid = r.astype(BF16).astype(F32)
    lo = (r - mid).astype(BF16).astype(F32)
    return hi, mid, lo


def _sigmoid(x):
    return 0.5 * jnp.tanh(0.5 * x) + 0.5


def _silu(x):
    half = 0.5 * x
    return half * (jnp.tanh(half) + 1.0)


def _proj_kernel(*refs, scaled_tiles, scale, normalise):
    if normalise:
        x_ref, g_ref, w_ref, b_ref, o_ref, hn_ref = refs

        @pl.when(pl.program_id(1) == 0)
        def _():
            x = x_ref[...]
            hn = x * lax.rsqrt(jnp.mean(x * x, axis=-1, keepdims=True) + RMS_EPS) * g_ref[...]
            hn_ref[...] = hn.astype(hn_ref.dtype)

        lhs = hn_ref[...]
    else:
        x_ref, w_ref, b_ref, o_ref = refs
        lhs = x_ref[...]
    acc = jnp.dot(lhs, w_ref[...].astype(BF16), preferred_element_type=F32) + b_ref[...]
    if scaled_tiles:
        acc = acc * jnp.where(pl.program_id(1) < scaled_tiles, scale, 1.0)
    o_ref[...] = acc.astype(o_ref.dtype)


def projection(x, w, layer, n_out, bias, *, norm_gain=None, scaled_cols=0, scale=1.0, tm=1024, tn=1024):
    m, k = x.shape
    assert m % tm == 0 and n_out % tn == 0 and scaled_cols % tn == 0 and n_out <= w.shape[1]
    assert w.shape[0] % k == 0 and layer < w.shape[0] // k
    normalise = norm_gain is not None
    rows = pl.BlockSpec((tm, k), lambda i, j: (i, 0))
    in_specs = [rows, pl.BlockSpec((k, tn), lambda i, j: (layer, j)), pl.BlockSpec((1, tn), lambda i, j: (0, j))]
    out_specs = pl.BlockSpec((tm, tn), lambda i, j: (i, j))
    out_shape = jax.ShapeDtypeStruct((m, n_out), BF16)
    operands = (x, w, bias)
    if normalise:
        in_specs.insert(1, pl.BlockSpec((1, k), lambda i, j: (0, 0)))
        out_specs, out_shape = [out_specs, rows], [out_shape, jax.ShapeDtypeStruct((m, k), BF16)]
        operands = (x, norm_gain, w, bias)
    return pl.pallas_call(
        functools.partial(_proj_kernel, scaled_tiles=scaled_cols // tn, scale=scale, normalise=normalise),
        grid=(m // tm, n_out // tn),
        in_specs=in_specs,
        out_specs=out_specs,
        out_shape=out_shape,
        compiler_params=_params("parallel", "arbitrary"),
        name="in_proj",
    )(*operands)


def _fgate_kernel(x_ref, wf_ref, bf_ref, o_ref, carry_ref):
    t = x_ref.shape[1]

    @pl.when(pl.program_id(1) == 0)
    def _():
        carry_ref[...] = jnp.zeros_like(carry_ref)

    z = jnp.dot(x_ref[0], wf_ref[...], preferred_element_type=F32) + bf_ref[...]
    log_f = jnp.minimum(z, 0.0) - jnp.log(1.0 + jnp.exp(-jnp.abs(z)))
    row = lax.broadcasted_iota(jnp.int32, (t, t), 0)
    col = lax.broadcasted_iota(jnp.int32, (t, t), 1)
    tri = (col <= row).astype(BF16)
    pieces = jnp.concatenate(_split3(log_f), axis=1).astype(BF16)
    local = jnp.dot(tri, pieces, preferred_element_type=F32)
    c = carry_ref[...] + local[:, :LANES] + local[:, LANES:2 * LANES] + local[:, 2 * LANES:]
    carry_ref[...] = c[t - 1:t, :]
    hi, mid, lo = _split3(c * (-LOG2E))
    piece = lax.broadcasted_iota(jnp.int32, (t, LANES), 1) % HEAD_LANES
    packed = jnp.where(piece == 0, hi, jnp.where(piece == 1, mid, jnp.where(piece == 2, lo, 0.0)))
    o_ref[0] = packed.astype(o_ref.dtype)


def forget_gate(hn, wf, bf, *, t=512):
    b, s, k = hn.shape
    assert s % t == 0
    return pl.pallas_call(
        _fgate_kernel,
        grid=(b, s // t),
        in_specs=[
            pl.BlockSpec((1, t, k), lambda bi, i: (bi, i, 0)),
            pl.BlockSpec((k, LANES), lambda bi, i: (0, 0)),
            pl.BlockSpec((1, LANES), lambda bi, i: (0, 0)),
        ],
        out_specs=pl.BlockSpec((1, t, LANES), lambda bi, i: (bi, i, 0)),
        out_shape=jax.ShapeDtypeStruct((b, s, LANES), BF16),
        scratch_shapes=[pltpu.VMEM((1, LANES), F32)],
        compiler_params=_params("parallel", "arbitrary"),
        name="forget_gate",
    )(hn, wf, bf)


def _attn_kernel(q_ref, k_ref, c_ref, v_ref, g_ref, o_ref,
                 kaug_ref, vaug_ref, qaug_ref, s0_ref, s1_ref, smax0_ref, smax1_ref, acc_ref, m_ref, *, tk):
    h = pl.program_id(1)
    s_len, d = q_ref.shape[1], q_ref.shape[2]
    tq = tk
    hk = tk // 2
    nq = s_len // tq
    n_pairs = nq * (nq + 1) // 2
    nt = (((1,), (1,)), ((), ()))

    kaug_ref[:, :d] = k_ref[0]
    kaug_ref[:, d:] = c_ref[0]
    vaug_ref[:d, :] = v_ref[0].astype(F32).T.astype(BF16)
    vaug_ref[d:, :] = jnp.ones((vaug_ref.shape[0] - d, s_len), BF16)
    col = lax.broadcasted_iota(jnp.int32, (s_len, LANES), 1)
    qaug_ref[:, :d] = q_ref[0]
    qaug_ref[:, d:] = ((col >= h * HEAD_LANES) & (col < h * HEAD_LANES + 3)).astype(BF16)
    s_ref, smax_ref = (s0_ref, s1_ref), (smax0_ref, smax1_ref)

    def causal(s):
        q_pos = lax.broadcasted_iota(jnp.int32, s.shape, 1)
        k_pos = lax.broadcasted_iota(jnp.int32, s.shape, 0)
        return jnp.where(k_pos <= q_pos, s, MASK_VALUE)

    def scores(k_start, k_rows, q_start, q_rows):
        return lax.dot_general(kaug_ref[pl.ds(k_start, k_rows), :], qaug_ref[pl.ds(q_start, q_rows), :], nt,
                               preferred_element_type=F32)

    def stage_a(slot, iq, t):
        s = scores(pl.multiple_of((t - 1) * tk, tk), tk, pl.multiple_of(iq * tq, tq), tq)
        s_ref[slot][...] = s
        smax_ref[slot][...] = jnp.max(s, axis=0, keepdims=True)

    def stage_a_diag(slot, iq):
        base = pl.multiple_of(iq * tq, tq)
        top = scores(base, hk, base, tq)
        top_left = causal(top[:, :hk])
        bottom = causal(scores(base + hk, hk, base + hk, hk))
        s_ref[slot][:hk, :hk] = top_left
        s_ref[slot][:hk, hk:] = top[:, hk:]
        s_ref[slot][hk:, hk:] = bottom
        smax_ref[slot][:, :hk] = jnp.max(top_left, axis=0, keepdims=True)
        smax_ref[slot][:, hk:] = jnp.maximum(jnp.max(top[:, hk:], axis=0, keepdims=True),
                                             jnp.max(bottom, axis=0, keepdims=True))

    def stage_bc(slot, iq, t):
        m_old = m_ref[...]
        m_new = jnp.maximum(m_old, smax_ref[slot][...])
        m_ref[...] = m_new
        p = jnp.exp2((s_ref[slot][...] - m_new).astype(BF16))
        pv = jnp.dot(vaug_ref[:, pl.ds(pl.multiple_of((t - 1) * tk, tk), tk)], p, preferred_element_type=F32)
        acc_ref[...] = jnp.exp2(m_old - m_new) * acc_ref[...] + pv

    def stage_bc_diag(slot, iq):
        base = pl.multiple_of(iq * tq, tq)
        m_new = smax_ref[slot][...]
        m_ref[...] = m_new
        p_left = jnp.exp2((s_ref[slot][:hk, :hk] - m_new[:, :hk]).astype(BF16))
        p_right = jnp.exp2((s_ref[slot][:, hk:] - m_new[:, hk:]).astype(BF16))
        acc_ref[:, :hk] = jnp.dot(vaug_ref[:, pl.ds(base, hk)], p_left, preferred_element_type=F32)
        acc_ref[:, hk:] = jnp.dot(vaug_ref[:, pl.ds(base, tk)], p_right, preferred_element_type=F32)

    def finalize(iq):
        rows = pl.ds(pl.multiple_of(iq * tq, tq), tq)
        o = (acc_ref[:d, :] / acc_ref[d:d + 1, :]).T
        o_ref[0, rows, :] = (o * _silu(g_ref[0, rows, :].astype(F32))).astype(o_ref.dtype)

    def close_and_open(slot, iq, t):
        stage_a_diag(1 - slot, iq + 1)
        if t is None:
            stage_bc_diag(slot, iq)
        else:
            stage_bc(slot, iq, t)
        finalize(iq)
        stage_a(slot, iq + 1, 1)
        stage_bc_diag(1 - slot, iq + 1)

    stage_a_diag(0, 0)
    if n_pairs == 1:
        stage_bc_diag(0, 0)
        finalize(0)
        return
    close_and_open(0, 0, None)

    def step(g, pair):
        iq, t = pair
        opens_tile = t == 0
        ends_tile = t == iq
        for slot in range(2):
            in_slot = g % 2 == slot

            @pl.when(in_slot & ends_tile)
            def _():
                close_and_open(slot, iq, t)

            @pl.when(in_slot & jnp.logical_not(opens_tile | ends_tile))
            def _():
                stage_a(1 - slot, iq, t + 1)
                stage_bc(slot, iq, t)

        return jnp.where(ends_tile, iq + 1, iq), jnp.where(ends_tile, 0, t + 1)

    iq, t = lax.fori_loop(2, n_pairs - 1, step, (jnp.int32(1), jnp.int32(1)))
    stage_bc((n_pairs - 1) % 2, iq, t)
    finalize(iq)


def fox_attention(qkvg, caug, *, tk=1024):
    b, s, _ = qkvg.shape
    hh, d = FOX_HEADS, FOX_HEAD_DIM
    assert s % tk == 0
    head_cols = lambda group: pl.BlockSpec((1, s, d), lambda bi, h: (bi, 0, group * hh + h))
    return pl.pallas_call(
        functools.partial(_attn_kernel, tk=tk),
        grid=(b, hh),
        in_specs=[head_cols(0), head_cols(1), pl.BlockSpec((1, s, LANES), lambda bi, h: (bi, 0, 0)),
                  head_cols(2), head_cols(3)],
        out_specs=head_cols(0),
        out_shape=jax.ShapeDtypeStruct((b, s, hh * d), BF16),
        scratch_shapes=[
            pltpu.VMEM((s, d + LANES), BF16),
            pltpu.VMEM((d + STRIP, s), BF16),
            pltpu.VMEM((s, d + LANES), BF16),
            pltpu.VMEM((tk, tk), F32), pltpu.VMEM((tk, tk), F32),
            pltpu.VMEM((1, tk), F32), pltpu.VMEM((1, tk), F32),
            pltpu.VMEM((d + STRIP, tk), F32),
            pltpu.VMEM((1, tk), F32),
        ],
        compiler_params=_params("parallel", "arbitrary"),
        name="fox_attention",
    )(qkvg, qkvg, caug, qkvg, qkvg)


def _conv_kernel(a_ref, b_ref, g_ref, ah_ref, bh_ref, dw_ref, dwb_ref, lng_ref, lnb_ref, o_ref,
                 u_ref, y_ref, *, rows):
    ts, c = o_ref.shape[1], o_ref.shape[2]
    first = pl.program_id(1) == 0

    def glu(a, b):
        return a.astype(F32) * _sigmoid(b.astype(F32))

    n_chunks = c // LANES

    def put_u(row, value):
        for ci in range(n_chunks):
            u_ref[ci, pl.ds(row, value.shape[0]), :] = value[:, ci * LANES:(ci + 1) * LANES]

    put_u(0, jnp.where(first, 0.0, glu(ah_ref[0], bh_ref[0])))

    def glu_strip(i, carry):
        r = pl.multiple_of(i * STRIP, STRIP)
        put_u(CONV_HALO + r, glu(a_ref[0, pl.ds(r, STRIP), :], b_ref[0, pl.ds(r, STRIP), :]))
        return carry

    lax.fori_loop(0, ts // STRIP, glu_strip, 0)

    shift = CONV_HALO - (CONV_KERNEL - 1)

    def chunk(ci, carry):
        lanes = pl.ds(pl.multiple_of(ci * LANES, LANES), LANES)
        taps = dw_ref[:, lanes]
        for r0 in range(0, ts, rows):
            u = [u_ref[ci, pl.ds(r0 + shift + m, SUBLANES, stride=SUBLANES), :]
                 for m in range(rows // SUBLANES + CONV_KERNEL - 1)]
            for j in range(rows // SUBLANES):
                acc = taps[0:1, :] * u[j]
                for kk in range(1, CONV_KERNEL):
                    acc = acc + taps[kk:kk + 1, :] * u[j + kk]
                y_ref[ci, pl.ds(r0 + j, SUBLANES, stride=SUBLANES), :] = acc
        return carry

    lax.fori_loop(0, n_chunks, chunk, 0)

    def norm_strip(i, carry):
        r = pl.multiple_of(i * STRIP, STRIP)
        y = jnp.concatenate([y_ref[ci, pl.ds(r, STRIP), :] for ci in range(n_chunks)], axis=1) + dwb_ref[...]
        mu = jnp.mean(y, axis=-1, keepdims=True)
        yc = y - mu
        var = jnp.mean(yc * yc, axis=-1, keepdims=True)
        z = yc * lax.rsqrt(var + LN_EPS) * lng_ref[...] + lnb_ref[...]
        gate = g_ref[0, pl.ds(r, STRIP), :].astype(F32)
        o_ref[0, pl.ds(r, STRIP), :] = (_silu(z) * _silu(gate)).astype(o_ref.dtype)
        return carry

    lax.fori_loop(0, ts // STRIP, norm_strip, 0, unroll=4)


def conv_module(abg, dw, dw_b, ln_g, ln_b, *, ts=256):
    b, s, c3 = abg.shape
    c = c3 // 3
    per = ts // CONV_HALO
    rows = SUBLANES * SUBLANES
    assert s % ts == 0 and ts % CONV_HALO == 0 and ts % rows == 0 and c % LANES == 0
    tile = lambda col: pl.BlockSpec((1, ts, c), lambda bi, i: (bi, i, col))
    halo = lambda col: pl.BlockSpec((1, CONV_HALO, c), lambda bi, i: (bi, jnp.maximum(i * per - 1, 0), col))
    vec = pl.BlockSpec((1, c), lambda bi, i: (0, 0))
    return pl.pallas_call(
        functools.partial(_conv_kernel, rows=rows),
        grid=(b, s // ts),
        in_specs=[tile(0), tile(1), tile(2), halo(0), halo(1),
                  pl.BlockSpec((CONV_KERNEL + 1, c), lambda bi, i: (0, 0)), vec, vec, vec],
        out_specs=pl.BlockSpec((1, ts, c), lambda bi, i: (bi, i, 0)),
        out_shape=jax.ShapeDtypeStruct((b, s, c), BF16),
        scratch_shapes=[pltpu.VMEM((c // LANES, ts + CONV_HALO, LANES), F32),
                        pltpu.VMEM((c // LANES, ts, LANES), F32)],
        compiler_params=_params("parallel", "arbitrary"),
        name="conv_module",
    )(abg, abg, abg, abg, abg, dw, dw_b, ln_g, ln_b)


def _out_kernel(y_ref, w_ref, h_ref, g_ref, *out_refs, last):
    h_new = h_ref[...] + jnp.dot(y_ref[...], w_ref[...], preferred_element_type=F32)
    normed = h_new * lax.rsqrt(jnp.mean(h_new * h_new, axis=-1, keepdims=True) + RMS_EPS) * g_ref[...]
    if last:
        out_refs[0][...] = normed
    else:
        out_refs[0][...] = h_new
        out_refs[1][...] = normed.astype(BF16)


def out_projection(y, w, layer, h, g, *, last, tm=512):
    m, k = y.shape
    d = w.shape[1]
    assert m % tm == 0 and w.shape[0] % k == 0 and layer < w.shape[0] // k
    row = lambda width: pl.BlockSpec((tm, width), lambda i: (i, 0))
    if last:
        out_specs, out_shape = row(d), jax.ShapeDtypeStruct((m, d), F32)
    else:
        out_specs = [row(d), row(d)]
        out_shape = [jax.ShapeDtypeStruct((m, d), F32), jax.ShapeDtypeStruct((m, d), BF16)]
    return pl.pallas_call(
        functools.partial(_out_kernel, last=last),
        grid=(m // tm,),
        in_specs=[row(k), pl.BlockSpec((k, d), lambda i: (layer, 0)), row(d),
                  pl.BlockSpec((1, d), lambda i: (0, 0))],
        out_specs=out_specs,
        out_shape=out_shape,
        compiler_params=_params("parallel"),
        name="out_proj",
    )(y, w, h, g)


def _forget_weights(w_f, b_f):
    lane = jnp.arange(LANES)
    spread = ((lane[None, :] // HEAD_LANES == jnp.arange(FOX_HEADS)[:, None])
              & (lane[None, :] % HEAD_LANES < 3)).astype(F32)
    wf = jnp.dot(w_f, spread, precision=lax.Precision.HIGHEST).astype(BF16)
    bf = jnp.dot(b_f[None, :], spread, precision=lax.Precision.HIGHEST)
    return wf, bf


def kernel(x, norm_g, fox_w_in, fox_b_f, fox_w_out, conv_w_in, conv_b_in, conv_dw, conv_dw_b, conv_ln_g,
           conv_ln_b, conv_w_out, final_norm_g):
    b, s, d = x.shape
    m = b * s
    h = x.reshape(m, d)
    fox_w = fox_w_in[:, :, :4 * FOX_WIDTH].astype(BF16).reshape(-1, 4 * FOX_WIDTH)
    fox_wf = fox_w_in[:, :, 4 * FOX_WIDTH:]
    conv_w = conv_w_in.reshape(-1, conv_w_in.shape[-1])
    fox_w_o = fox_w_out.astype(BF16).reshape(-1, d)
    conv_w_o = conv_w_out.astype(BF16).reshape(-1, d)
    for i in range(DEPTH):
        j = i // 2
        if i % 2 == 0:
            wf, bf = _forget_weights(fox_wf[j], fox_b_f[j])
            q_scale = dict(scaled_cols=FOX_WIDTH, scale=FOX_HEAD_DIM ** -0.5 * LOG2E)
            no_bias = jnp.zeros((1, 4 * FOX_WIDTH), F32)
            if i == 0:
                qkvg, hn = projection(h, fox_w, j, 4 * FOX_WIDTH, no_bias, norm_gain=norm_g[0][None, :], **q_scale)
            else:
                qkvg = projection(hn, fox_w, j, 4 * FOX_WIDTH, no_bias, tn=2048, **q_scale)
            caug = forget_gate(hn.reshape(b, s, d), wf, bf)
            y = fox_attention(qkvg.reshape(b, s, -1), caug).reshape(m, FOX_WIDTH)
            w_out = fox_w_o
        else:
            abg = projection(hn, conv_w, j, 3 * CONV_CHANNELS, conv_b_in[j][None, :]).reshape(b, s, -1)
            dw = jnp.concatenate([conv_dw[j], jnp.zeros((1, CONV_CHANNELS), F32)], axis=0)
            y = conv_module(abg, dw, conv_dw_b[j][None, :], conv_ln_g[j][None, :],
                            conv_ln_b[j][None, :]).reshape(m, CONV_CHANNELS)
            w_out = conv_w_o
        if i + 1 < DEPTH:
            h, hn = out_projection(y, w_out, j, h, norm_g[i + 1][None, :], last=False)
        else:
            out = out_projection(y, w_out, j, h, final_norm_g[None, :], last=True)
    return out.reshape(b, s, d)
```

```python
import functools
import math

import jax
import jax.numpy as jnp
from jax import lax
from jax.experimental import pallas as pl
from jax.experimental.pallas import tpu as pltpu

D_MODEL = 2048
DEPTH = 4
FOX_HEADS = 16
FOX_HEAD_DIM = 128
FOX_WIDTH = FOX_HEADS * FOX_HEAD_DIM
CONV_CHANNELS = D_MODEL
CONV_KERNEL = 31
RMS_EPS = 1e-6
LN_EPS = 1e-5

LANES = 128
SUBLANES = 8
LOG2E = 1.4426950408889634
VMEM_LIMIT = 56 * 1024 * 1024
HEAD_LANES = LANES // FOX_HEADS
MASK_VALUE = -1e30
CONV_HALO = 32
STRIP = 16

BF16 = jnp.bfloat16
F32 = jnp.float32


def _params(*sem):
    return pltpu.CompilerParams(dimension_semantics=sem, vmem_limit_bytes=VMEM_LIMIT)


def _split3(x):
    hi = x.astype(BF16).astype(F32)
    r = x - hi
    mid = r.astype(BF16).astype(F32)
    lo = (r - mid).astype(BF16).astype(F32)
    return hi, mid, lo


def _sigmoid(x):
    return 0.5 * jnp.tanh(0.5 * x) + 0.5


def _silu(x):
    half = 0.5 * x
    return half * (jnp.tanh(half) + 1.0)


def _proj_kernel(*refs, scaled_tiles, scale, normalise):
    if normalise:
        x_ref, g_ref, w_ref, b_ref, o_ref, hn_ref = refs

        @pl.when(pl.program_id(1) == 0)
        def _():
            x = x_ref[...]
            hn = x * lax.rsqrt(jnp.mean(x * x, axis=-1, keepdims=True) + RMS_EPS) * g_ref[...]
            hn_ref[...] = hn.astype(hn_ref.dtype)

        lhs = hn_ref[...]
    else:
        x_ref, w_ref, b_ref, o_ref = refs
        lhs = x_ref[...]
    acc = jnp.dot(lhs, w_ref[...].astype(BF16), preferred_element_type=F32) + b_ref[...]
    if scaled_tiles:
        acc = acc * jnp.where(pl.program_id(1) < scaled_tiles, scale, 1.0)
    o_ref[...] = acc.astype(o_ref.dtype)


def projection(x, w, layer, n_out, bias, *, norm_gain=None, scaled_cols=0, scale=1.0, tm=1024, tn=1024):
    m, k = x.shape
    assert m % tm == 0 and n_out % tn == 0 and scaled_cols % tn == 0 and n_out <= w.shape[1]
    assert w.shape[0] % k == 0 and layer < w.shape[0] // k
    normalise = norm_gain is not None
    rows = pl.BlockSpec((tm, k), lambda i, j: (i, 0))
    in_specs = [rows, pl.BlockSpec((k, tn), lambda i, j: (layer, j)), pl.BlockSpec((1, tn), lambda i, j: (0, j))]
    out_specs = pl.BlockSpec((tm, tn), lambda i, j: (i, j))
    out_shape = jax.ShapeDtypeStruct((m, n_out), BF16)
    operands = (x, w, bias)
    if normalise:
        in_specs.insert(1, pl.BlockSpec((1, k), lambda i, j: (0, 0)))
        out_specs, out_shape = [out_specs, rows], [out_shape, jax.ShapeDtypeStruct((m, k), BF16)]
        operands = (x, norm_gain, w, bias)
    return pl.pallas_call(
        functools.partial(_proj_kernel, scaled_tiles=scaled_cols // tn, scale=scale, normalise=normalise),
        grid=(m // tm, n_out // tn),
        in_specs=in_specs,
        out_specs=out_specs,
        out_shape=out_shape,
        compiler_params=_params("parallel", "arbitrary"),
        name="in_proj",
    )(*operands)


def _fgate_kernel(x_ref, wf_ref, bf_ref, o_ref, carry_ref):
    t = x_ref.shape[1]

    @pl.when(pl.program_id(1) == 0)
    def _():
        carry_ref[...] = jnp.zeros_like(carry_ref)

    z = jnp.dot(x_ref[0], wf_ref[...], preferred_element_type=F32) + bf_ref[...]
    log_f = jnp.minimum(z, 0.0) - jnp.log(1.0 + jnp.exp(-jnp.abs(z)))
    row = lax.broadcasted_iota(jnp.int32, (t, t), 0)
    col = lax.broadcasted_iota(jnp.int32, (t, t), 1)
    tri = (col <= row).astype(BF16)
    pieces = jnp.concatenate(_split3(log_f), axis=1).astype(BF16)
    local = jnp.dot(tri, pieces, preferred_element_type=F32)
    c = carry_ref[...] + local[:, :LANES] + local[:, LANES:2 * LANES] + local[:, 2 * LANES:]
    carry_ref[...] = c[t - 1:t, :]
    hi, mid, lo = _split3(c * (-LOG2E))
    piece = lax.broadcasted_iota(jnp.int32, (t, LANES), 1) % HEAD_LANES
    packed = jnp.where(piece == 0, hi, jnp.where(piece == 1, mid, jnp.where(piece == 2, lo, 0.0)))
    o_ref[0] = packed.astype(o_ref.dtype)


def forget_gate(hn, wf, bf, *, t=512):
    b, s, k = hn.shape
    assert s % t == 0
    return pl.pallas_call(
        _fgate_kernel,
        grid=(b, s // t),
        in_specs=[
            pl.BlockSpec((1, t, k), lambda bi, i: (bi, i, 0)),
            pl.BlockSpec((k, LANES), lambda bi, i: (0, 0)),
            pl.BlockSpec((1, LANES), lambda bi, i: (0, 0)),
        ],
        out_specs=pl.BlockSpec((1, t, LANES), lambda bi, i: (bi, i, 0)),
        out_shape=jax.ShapeDtypeStruct((b, s, LANES), BF16),
        scratch_shapes=[pltpu.VMEM((1, LANES), F32)],
        compiler_params=_params("parallel", "arbitrary"),
        name="forget_gate",
    )(hn, wf, bf)


def _attn_kernel(q_ref, k_ref, c_ref, v_ref, g_ref, o_ref,
                 kaug_ref, vaug_ref, qaug_ref, s0_ref, s1_ref, smax0_ref, smax1_ref, acc_ref, m_ref, *, tk):
    h = pl.program_id(1)
    s_len, d = q_ref.shape[1], q_ref.shape[2]
    tq = tk
    hk = tk // 2
    nq = s_len // tq
    n_pairs = nq * (nq + 1) // 2
    nt = (((1,), (1,)), ((), ()))

    kaug_ref[:, :d] = k_ref[0]
    kaug_ref[:, d:] = c_ref[0]
    vaug_ref[:d, :] = v_ref[0].astype(F32).T.astype(BF16)
    vaug_ref[d:, :] = jnp.ones((vaug_ref.shape[0] - d, s_len), BF16)
    col = lax.broadcasted_iota(jnp.int32, (s_len, LANES), 1)
    qaug_ref[:, :d] = q_ref[0]
    qaug_ref[:, d:] = ((col >= h * HEAD_LANES) & (col < h * HEAD_LANES + 3)).astype(BF16)
    s_ref, smax_ref = (s0_ref, s1_ref), (smax0_ref, smax1_ref)

    def causal(s):
        q_pos = lax.broadcasted_iota(jnp.int32, s.shape, 1)
        k_pos = lax.broadcasted_iota(jnp.int32, s.shape, 0)
        return jnp.where(k_pos <= q_pos, s, MASK_VALUE)

    def scores(k_start, k_rows, q_start, q_rows):
        return lax.dot_general(kaug_ref[pl.ds(k_start, k_rows), :], qaug_ref[pl.ds(q_start, q_rows), :], nt,
                               preferred_element_type=F32)

    def stage_a(slot, iq, t):
        s = scores(pl.multiple_of((t - 1) * tk, tk), tk, pl.multiple_of(iq * tq, tq), tq)
        s_ref[slot][...] = s
        smax_ref[slot][...] = jnp.max(s, axis=0, keepdims=True)

    def stage_a_diag(slot, iq):
        base = pl.multiple_of(iq * tq, tq)
        top = scores(base, hk, base, tq)
        top_left = causal(top[:, :hk])
        bottom = causal(scores(base + hk, hk, base + hk, hk))
        s_ref[slot][:hk, :hk] = top_left
        s_ref[slot][:hk, hk:] = top[:, hk:]
        s_ref[slot][hk:, hk:] = bottom
        smax_ref[slot][:, :hk] = jnp.max(top_left, axis=0, keepdims=True)
        smax_ref[slot][:, hk:] = jnp.maximum(jnp.max(top[:, hk:], axis=0, keepdims=True),
                                             jnp.max(bottom, axis=0, keepdims=True))

    def stage_bc(slot, iq, t):
        m_old = m_ref[...]
        m_new = jnp.maximum(m_old, smax_ref[slot][...])
        m_ref[...] = m_new
        p = jnp.exp2((s_ref[slot][...] - m_new).astype(BF16))
        pv = jnp.dot(vaug_ref[:, pl.ds(pl.multiple_of((t - 1) * tk, tk), tk)], p, preferred_element_type=F32)
        acc_ref[...] = jnp.exp2(m_old - m_new) * acc_ref[...] + pv

    def stage_bc_diag(slot, iq):
        base = pl.multiple_of(iq * tq, tq)
        m_new = smax_ref[slot][...]
        m_ref[...] = m_new
        p_left = jnp.exp2((s_ref[slot][:hk, :hk] - m_new[:, :hk]).astype(BF16))
        p_right = jnp.exp2((s_ref[slot][:, hk:] - m_new[:, hk:]).astype(BF16))
        acc_ref[:, :hk] = jnp.dot(vaug_ref[:, pl.ds(base, hk)], p_left, preferred_element_type=F32)
        acc_ref[:, hk:] = jnp.dot(vaug_ref[:, pl.ds(base, tk)], p_right, preferred_element_type=F32)

    def finalize(iq):
        rows = pl.ds(pl.multiple_of(iq * tq, tq), tq)
        o = (acc_ref[:d, :] / acc_ref[d:d + 1, :]).T
        o_ref[0, rows, :] = (o * _silu(g_ref[0, rows, :].astype(F32))).astype(o_ref.dtype)

    def close_and_open(slot, iq, t):
        stage_a_diag(1 - slot, iq + 1)
        if t is None:
            stage_bc_diag(slot, iq)
        else:
            stage_bc(slot, iq, t)
        finalize(iq)
        stage_a(slot, iq + 1, 1)
        stage_bc_diag(1 - slot, iq + 1)

    stage_a_diag(0, 0)
    if n_pairs == 1:
        stage_bc_diag(0, 0)
        finalize(0)
        return
    close_and_open(0, 0, None)

    def step(g, pair):
        iq, t = pair
        opens_tile = t == 0
        ends_tile = t == iq
        for slot in range(2):
            in_slot = g % 2 == slot

            @pl.when(in_slot & ends_tile)
            def _():
                close_and_open(slot, iq, t)

            @pl.when(in_slot & jnp.logical_not(opens_tile | ends_tile))
            def _():
                stage_a(1 - slot, iq, t + 1)
                stage_bc(slot, iq, t)

        return jnp.where(ends_tile, iq + 1, iq), jnp.where(ends_tile, 0, t + 1)

    iq, t = lax.fori_loop(2, n_pairs - 1, step, (jnp.int32(1), jnp.int32(1)))
    stage_bc((n_pairs - 1) % 2, iq, t)
    finalize(iq)


def fox_attention(qkvg, caug, *, tk=1024):
    b, s, _ = qkvg.shape
    hh, d = FOX_HEADS, FOX_HEAD_DIM
    assert s % tk == 0
    head_cols = lambda group: pl.BlockSpec((1, s, d), lambda bi, h: (bi, 0, group * hh + h))
    return pl.pallas_call(
        functools.partial(_attn_kernel, tk=tk),
        grid=(b, hh),
        in_specs=[head_cols(0), head_cols(1), pl.BlockSpec((1, s, LANES), lambda bi, h: (bi, 0, 0)),
                  head_cols(2), head_cols(3)],
        out_specs=head_cols(0),
        out_shape=jax.ShapeDtypeStruct((b, s, hh * d), BF16),
        scratch_shapes=[
            pltpu.VMEM((s, d + LANES), BF16),
            pltpu.VMEM((d + STRIP, s), BF16),
            pltpu.VMEM((s, d + LANES), BF16),
            pltpu.VMEM((tk, tk), F32), pltpu.VMEM((tk, tk), F32),
            pltpu.VMEM((1, tk), F32), pltpu.VMEM((1, tk), F32),
            pltpu.VMEM((d + STRIP, tk), F32),
            pltpu.VMEM((1, tk), F32),
        ],
        compiler_params=_params("parallel", "arbitrary"),
        name="fox_attention",
    )(qkvg, qkvg, caug, qkvg, qkvg)


def _conv_kernel(a_ref, b_ref, g_ref, ah_ref, bh_ref, dw_ref, dwb_ref, lng_ref, lnb_ref, o_ref,
                 u_ref, y_ref, *, rows):
    ts, c = o_ref.shape[1], o_ref.shape[2]
    first = pl.program_id(1) == 0

    def glu(a, b):
        return a.astype(F32) * _sigmoid(b.astype(F32))

    n_chunks = c // LANES

    def put_u(row, value):
        for ci in range(n_chunks):
            u_ref[ci, pl.ds(row, value.shape[0]), :] = value[:, ci * LANES:(ci + 1) * LANES]

    put_u(0, jnp.where(first, 0.0, glu(ah_ref[0], bh_ref[0])))

    def glu_strip(i, carry):
        r = pl.multiple_of(i * STRIP, STRIP)
        put_u(CONV_HALO + r, glu(a_ref[0, pl.ds(r, STRIP), :], b_ref[0, pl.ds(r, STRIP), :]))
        return carry

    lax.fori_loop(0, ts // STRIP, glu_strip, 0, unroll=2)

    shift = CONV_HALO - (CONV_KERNEL - 1)

    def chunk(ci, carry):
        lanes = pl.ds(pl.multiple_of(ci * LANES, LANES), LANES)
        taps = dw_ref[:, lanes]
        for r0 in range(0, ts, rows):
            u = [u_ref[ci, pl.ds(r0 + shift + m, SUBLANES, stride=SUBLANES), :]
                 for m in range(rows // SUBLANES + CONV_KERNEL - 1)]
            for j in range(rows // SUBLANES):
                acc = taps[0:1, :] * u[j]
                for kk in range(1, CONV_KERNEL):
                    acc = acc + taps[kk:kk + 1, :] * u[j + kk]
                y_ref[ci, pl.ds(r0 + j, SUBLANES, stride=SUBLANES), :] = acc
        return carry

    lax.fori_loop(0, n_chunks, chunk, 0)

    def norm_strip(i, carry):
        r = pl.multiple_of(i * STRIP, STRIP)
        y = jnp.concatenate([y_ref[ci, pl.ds(r, STRIP), :] for ci in range(n_chunks)], axis=1) + dwb_ref[...]
        mu = jnp.mean(y, axis=-1, keepdims=True)
        yc = y - mu
        var = jnp.mean(yc * yc, axis=-1, keepdims=True)
        z = yc * lax.rsqrt(var + LN_EPS) * lng_ref[...] + lnb_ref[...]
        gate = g_ref[0, pl.ds(r, STRIP), :].astype(F32)
        o_ref[0, pl.ds(r, STRIP), :] = (_silu(z) * _silu(gate)).astype(o_ref.dtype)
        return carry

    lax.fori_loop(0, ts // STRIP, norm_strip, 0, unroll=16)


def conv_module(abg, dw, dw_b, ln_g, ln_b, *, ts=256):
    b, s, c3 = abg.shape
    c = c3 // 3
    per = ts // CONV_HALO
    rows = SUBLANES * SUBLANES
    assert s % ts == 0 and ts % CONV_HALO == 0 and ts % rows == 0 and c % LANES == 0
    tile = lambda col: pl.BlockSpec((1, ts, c), lambda bi, i: (bi, i, col))
    halo = lambda col: pl.BlockSpec((1, CONV_HALO, c), lambda bi, i: (bi, jnp.maximum(i * per - 1, 0), col))
    vec = pl.BlockSpec((1, c), lambda bi, i: (0, 0))
    return pl.pallas_call(
        functools.partial(_conv_kernel, rows=rows),
        grid=(b, s // ts),
        in_specs=[tile(0), tile(1), tile(2), halo(0), halo(1),
                  pl.BlockSpec((CONV_KERNEL + 1, c), lambda bi, i: (0, 0)), vec, vec, vec],
        out_specs=pl.BlockSpec((1, ts, c), lambda bi, i: (bi, i, 0)),
        out_shape=jax.ShapeDtypeStruct((b, s, c), BF16),
        scratch_shapes=[pltpu.VMEM((c // LANES, ts + CONV_HALO, LANES), F32),
                        pltpu.VMEM((c // LANES, ts, LANES), F32)],
        compiler_params=_params("parallel", "arbitrary"),
        name="conv_module",
    )(abg, abg, abg, abg, abg, dw, dw_b, ln_g, ln_b)


def _out_kernel(y_ref, w_ref, h_ref, g_ref, *out_refs, last):
    h_new = h_ref[...] + jnp.dot(y_ref[...], w_ref[...], preferred_element_type=F32)
    normed = h_new * lax.rsqrt(jnp.mean(h_new * h_new, axis=-1, keepdims=True) + RMS_EPS) * g_ref[...]
    if last:
        out_refs[0][...] = normed
    else:
        out_refs[0][...] = h_new
        out_refs[1][...] = normed.astype(BF16)


def out_projection(y, w, layer, h, g, *, last, tm=512):
    m, k = y.shape
    d = w.shape[1]
    assert m % tm == 0 and w.shape[0] % k == 0 and layer < w.shape[0] // k
    row = lambda width: pl.BlockSpec((tm, width), lambda i: (i, 0))
    if last:
        out_specs, out_shape = row(d), jax.ShapeDtypeStruct((m, d), F32)
    else:
        out_specs = [row(d), row(d)]
        out_shape = [jax.ShapeDtypeStruct((m, d), F32), jax.ShapeDtypeStruct((m, d), BF16)]
    return pl.pallas_call(
        functools.partial(_out_kernel, last=last),
        grid=(m // tm,),
        in_specs=[row(k), pl.BlockSpec((k, d), lambda i: (layer, 0)), row(d),
                  pl.BlockSpec((1, d), lambda i: (0, 0))],
        out_specs=out_specs,
        out_shape=out_shape,
        compiler_params=_params("parallel"),
        name="out_proj",
    )(y, w, h, g)


def _forget_weights(w_f, b_f):
    lane = jnp.arange(LANES)
    spread = ((lane[None, :] // HEAD_LANES == jnp.arange(FOX_HEADS)[:, None])
              & (lane[None, :] % HEAD_LANES < 3)).astype(F32)
    wf = jnp.dot(w_f, spread, precision=lax.Precision.HIGHEST).astype(BF16)
    bf = jnp.dot(b_f[None, :], spread, precision=lax.Precision.HIGHEST)
    return wf, bf


def kernel(x, norm_g, fox_w_in, fox_b_f, fox_w_out, conv_w_in, conv_b_in, conv_dw, conv_dw_b, conv_ln_g,
           conv_ln_b, conv_w_out, final_norm_g):
    b, s, d = x.shape
    m = b * s
    h = x.reshape(m, d)
    fox_w = fox_w_in[:, :, :4 * FOX_WIDTH].astype(BF16).reshape(-1, 4 * FOX_WIDTH)
    fox_wf = fox_w_in[:, :, 4 * FOX_WIDTH:]
    conv_w = conv_w_in.reshape(-1, conv_w_in.shape[-1])
    fox_w_o = fox_w_out.astype(BF16).reshape(-1, d)
    conv_w_o = conv_w_out.astype(BF16).reshape(-1, d)
    for i in range(DEPTH):
        j = i // 2
        if i % 2 == 0:
            wf, bf = _forget_weights(fox_wf[j], fox_b_f[j])
            q_scale = dict(scaled_cols=FOX_WIDTH, scale=FOX_HEAD_DIM ** -0.5 * LOG2E)
            no_bias = jnp.zeros((1, 4 * FOX_WIDTH), F32)
            if i == 0:
                qkvg, hn = projection(h, fox_w, j, 4 * FOX_WIDTH, no_bias, norm_gain=norm_g[0][None, :], **q_scale)
            else:
                qkvg = projection(hn, fox_w, j, 4 * FOX_WIDTH, no_bias, tn=2048, **q_scale)
            caug = forget_gate(hn.reshape(b, s, d), wf, bf)
            y = fox_attention(qkvg.reshape(b, s, -1), caug).reshape(m, FOX_WIDTH)
            w_out = fox_w_o
        else:
            abg = projection(hn, conv_w, j, 3 * CONV_CHANNELS, conv_b_in[j][None, :]).reshape(b, s, -1)
            dw = jnp.concatenate([conv_dw[j], jnp.zeros((1, CONV_CHANNELS), F32)], axis=0)
            y = conv_module(abg, dw, conv_dw_b[j][None, :], conv_ln_g[j][None, :],
                            conv_ln_b[j][None, :]).reshape(m, CONV_CHANNELS)
            w_out = conv_w_o
        if i + 1 < DEPTH:
            h, hn = out_projection(y, w_out, j, h, norm_g[i + 1][None, :], last=False)
        else:
            out = out_projection(y, w_out, j, h, final_norm_g[None, :], last=True)
    return out.reshape(b, s, d)
```

```python
import functools
import math

import jax
import jax.numpy as jnp
from jax import lax
from jax.experimental import pallas as pl
from jax.experimental.pallas import tpu as pltpu

D_MODEL = 2048
DEPTH = 4
FOX_HEADS = 16
FOX_HEAD_DIM = 128
FOX_WIDTH = FOX_HEADS * FOX_HEAD_DIM
CONV_CHANNELS = D_MODEL
CONV_KERNEL = 31
RMS_EPS = 1e-6
LN_EPS = 1e-5

LANES = 128
SUBLANES = 8
LOG2E = 1.4426950408889634
VMEM_LIMIT = 56 * 1024 * 1024
HEAD_LANES = LANES // FOX_HEADS
MASK_VALUE = -1e30
CONV_HALO = 32
STRIP = 16
RING = 3

BF16 = jnp.bfloat16
F32 = jnp.float32


def _params(*sem):
    return pltpu.CompilerParams(dimension_semantics=sem, vmem_limit_bytes=VMEM_LIMIT)


def _split3(x):
    hi = x.astype(BF16).astype(F32)
    r = x - hi
    mid = r.astype(BF16).astype(F32)
    lo = (r - mid).astype(BF16).astype(F32)
    return hi, mid, lo


def _sigmoid(x):
    return 0.5 * jnp.tanh(0.5 * x) + 0.5


def _silu(x):
    half = 0.5 * x
    return half * (jnp.tanh(half) + 1.0)


def _proj_kernel(*refs, scaled_tiles, scale, normalise, layer):
    if normalise:
        x_ref, g_ref, w_ref, b_ref, o_ref, hn_ref = refs

        @pl.when(pl.program_id(1) == 0)
        def _():
            x = x_ref[...]
            hn = x * lax.rsqrt(jnp.mean(x * x, axis=-1, keepdims=True) + RMS_EPS) * g_ref[...]
            hn_ref[...] = hn.astype(hn_ref.dtype)

        lhs = hn_ref[...]
        w = w_ref[...]
    else:
        x_ref, w_hbm, b_ref, o_ref, w_buf, w_sem = refs
        lhs = x_ref[...]
        n_cols = pl.num_programs(1)
        f = pl.program_id(0) * n_cols + pl.program_id(1)
        total = pl.num_programs(0) * n_cols
        k, tn = w_buf.shape[1], w_buf.shape[2]

        def tile_copy(step):
            col = pl.multiple_of((step % n_cols) * tn, tn)
            slot = step % RING
            return pltpu.make_async_copy(w_hbm.at[pl.ds(layer * k, k), pl.ds(col, tn)], w_buf.at[slot],
                                         w_sem.at[slot])

        @pl.when(f == 0)
        def _():
            for step in range(RING - 1):
                tile_copy(step).start()

        @pl.when(f + RING - 1 < total)
        def _():
            tile_copy(f + RING - 1).start()

        tile_copy(f).wait()
        w = w_buf[f % RING]
    acc = jnp.dot(lhs, w.astype(BF16), preferred_element_type=F32) + b_ref[...]
    if scaled_tiles:
        acc = acc * jnp.where(pl.program_id(1) < scaled_tiles, scale, 1.0)
    o_ref[...] = acc.astype(o_ref.dtype)


def projection(x, w, layer, n_out, bias, *, norm_gain=None, scaled_cols=0, scale=1.0, tm=1024, tn=1024):
    m, k = x.shape
    assert m % tm == 0 and n_out % tn == 0 and scaled_cols % tn == 0 and n_out <= w.shape[1]
    assert w.shape[0] % k == 0 and layer < w.shape[0] // k
    normalise = norm_gain is not None
    rows = pl.BlockSpec((tm, k), lambda i, j: (i, 0))
    bias_spec = pl.BlockSpec((1, tn), lambda i, j: (0, j))
    out_specs = pl.BlockSpec((tm, tn), lambda i, j: (i, j))
    out_shape = jax.ShapeDtypeStruct((m, n_out), BF16)
    if normalise:
        in_specs = [rows, pl.BlockSpec((1, k), lambda i, j: (0, 0)),
                    pl.BlockSpec((k, tn), lambda i, j: (layer, j)), bias_spec]
        out_specs, out_shape = [out_specs, rows], [out_shape, jax.ShapeDtypeStruct((m, k), BF16)]
        operands, scratch = (x, norm_gain, w, bias), []
    else:
        assert (m // tm) * (n_out // tn) >= RING
        in_specs = [rows, pl.BlockSpec(memory_space=pl.ANY), bias_spec]
        operands = (x, w, bias)
        scratch = [pltpu.VMEM((RING, k, tn), w.dtype), pltpu.SemaphoreType.DMA((RING,))]
    return pl.pallas_call(
        functools.partial(_proj_kernel, scaled_tiles=scaled_cols // tn, scale=scale, normalise=normalise,
                          layer=layer),
        grid=(m // tm, n_out // tn),
        in_specs=in_specs,
        out_specs=out_specs,
        out_shape=out_shape,
        scratch_shapes=scratch,
        compiler_params=_params("arbitrary", "arbitrary"),
        name="in_proj",
    )(*operands)


def _fgate_kernel(x_ref, wf_ref, bf_ref, o_ref, carry_ref):
    t = x_ref.shape[1]

    @pl.when(pl.program_id(1) == 0)
    def _():
        carry_ref[...] = jnp.zeros_like(carry_ref)

    z = jnp.dot(x_ref[0], wf_ref[...], preferred_element_type=F32) + bf_ref[...]
    log_f = jnp.minimum(z, 0.0) - jnp.log(1.0 + jnp.exp(-jnp.abs(z)))
    row = lax.broadcasted_iota(jnp.int32, (t, t), 0)
    col = lax.broadcasted_iota(jnp.int32, (t, t), 1)
    tri = (col <= row).astype(BF16)
    pieces = jnp.concatenate(_split3(log_f), axis=1).astype(BF16)
    local = jnp.dot(tri, pieces, preferred_element_type=F32)
    c = carry_ref[...] + local[:, :LANES] + local[:, LANES:2 * LANES] + local[:, 2 * LANES:]
    carry_ref[...] = c[t - 1:t, :]
    hi, mid, lo = _split3(c * (-LOG2E))
    piece = lax.broadcasted_iota(jnp.int32, (t, LANES), 1) % HEAD_LANES
    packed = jnp.where(piece == 0, hi, jnp.where(piece == 1, mid, jnp.where(piece == 2, lo, 0.0)))
    o_ref[0] = packed.astype(o_ref.dtype)


def forget_gate(hn, wf, bf, *, t=512):
    b, s, k = hn.shape
    assert s % t == 0
    return pl.pallas_call(
        _fgate_kernel,
        grid=(b, s // t),
        in_specs=[
            pl.BlockSpec((1, t, k), lambda bi, i: (bi, i, 0)),
            pl.BlockSpec((k, LANES), lambda bi, i: (0, 0)),
            pl.BlockSpec((1, LANES), lambda bi, i: (0, 0)),
        ],
        out_specs=pl.BlockSpec((1, t, LANES), lambda bi, i: (bi, i, 0)),
        out_shape=jax.ShapeDtypeStruct((b, s, LANES), BF16),
        scratch_shapes=[pltpu.VMEM((1, LANES), F32)],
        compiler_params=_params("parallel", "arbitrary"),
        name="forget_gate",
    )(hn, wf, bf)


def _attn_kernel(q_ref, k_ref, c_ref, v_ref, g_ref, o_ref,
                 kaug_ref, vaug_ref, qaug_ref, s0_ref, s1_ref, smax0_ref, smax1_ref, acc_ref, m_ref, *, tk):
    h = pl.program_id(1)
    s_len, d = q_ref.shape[1], q_ref.shape[2]
    tq = tk
    hk = tk // 2
    nq = s_len // tq
    n_pairs = nq * (nq + 1) // 2
    nt = (((1,), (1,)), ((), ()))

    kaug_ref[:, :d] = k_ref[0]
    kaug_ref[:, d:] = c_ref[0]
    vaug_ref[:d, :] = v_ref[0].astype(F32).T.astype(BF16)
    vaug_ref[d:, :] = jnp.ones((vaug_ref.shape[0] - d, s_len), BF16)
    col = lax.broadcasted_iota(jnp.int32, (s_len, LANES), 1)
    qaug_ref[:, :d] = q_ref[0]
    qaug_ref[:, d:] = ((col >= h * HEAD_LANES) & (col < h * HEAD_LANES + 3)).astype(BF16)
    s_ref, smax_ref = (s0_ref, s1_ref), (smax0_ref, smax1_ref)

    def causal(s):
        q_pos = lax.broadcasted_iota(jnp.int32, s.shape, 1)
        k_pos = lax.broadcasted_iota(jnp.int32, s.shape, 0)
        return jnp.where(k_pos <= q_pos, s, MASK_VALUE)

    def scores(k_start, k_rows, q_start, q_rows):
        return lax.dot_general(kaug_ref[pl.ds(k_start, k_rows), :], qaug_ref[pl.ds(q_start, q_rows), :], nt,
                               preferred_element_type=F32)

    def stage_a(slot, iq, t):
        s = scores(pl.multiple_of((t - 1) * tk, tk), tk, pl.multiple_of(iq * tq, tq), tq)
        s_ref[slot][...] = s
        smax_ref[slot][...] = jnp.max(s, axis=0, keepdims=True)

    def stage_a_diag(slot, iq):
        base = pl.multiple_of(iq * tq, tq)
        top = scores(base, hk, base, tq)
        top_left = causal(top[:, :hk])
        bottom = causal(scores(base + hk, hk, base + hk, hk))
        s_ref[slot][:hk, :hk] = top_left
        s_ref[slot][:hk, hk:] = top[:, hk:]
        s_ref[slot][hk:, hk:] = bottom
        smax_ref[slot][:, :hk] = jnp.max(top_left, axis=0, keepdims=True)
        smax_ref[slot][:, hk:] = jnp.maximum(jnp.max(top[:, hk:], axis=0, keepdims=True),
                                             jnp.max(bottom, axis=0, keepdims=True))

    def stage_bc(slot, iq, t):
        m_old = m_ref[...]
        m_new = jnp.maximum(m_old, smax_ref[slot][...])
        m_ref[...] = m_new
        p = jnp.exp2((s_ref[slot][...] - m_new).astype(BF16))
        pv = jnp.dot(vaug_ref[:, pl.ds(pl.multiple_of((t - 1) * tk, tk), tk)], p, preferred_element_type=F32)
        acc_ref[...] = jnp.exp2(m_old - m_new) * acc_ref[...] + pv

    def stage_bc_diag(slot, iq):
        base = pl.multiple_of(iq * tq, tq)
        m_new = smax_ref[slot][...]
        m_ref[...] = m_new
        p_left = jnp.exp2((s_ref[slot][:hk, :hk] - m_new[:, :hk]).astype(BF16))
        p_right = jnp.exp2((s_ref[slot][:, hk:] - m_new[:, hk:]).astype(BF16))
        acc_ref[:, :hk] = jnp.dot(vaug_ref[:, pl.ds(base, hk)], p_left, preferred_element_type=F32)
        acc_ref[:, hk:] = jnp.dot(vaug_ref[:, pl.ds(base, tk)], p_right, preferred_element_type=F32)

    def finalize(iq):
        rows = pl.ds(pl.multiple_of(iq * tq, tq), tq)
        o = (acc_ref[:d, :] / acc_ref[d:d + 1, :]).T
        o_ref[0, rows, :] = (o * _silu(g_ref[0, rows, :].astype(F32))).astype(o_ref.dtype)

    def close_and_open(slot, iq, t):
        stage_a_diag(1 - slot, iq + 1)
        if t is None:
            stage_bc_diag(slot, iq)
        else:
            stage_bc(slot, iq, t)
        finalize(iq)
        stage_a(slot, iq + 1, 1)
        stage_bc_diag(1 - slot, iq + 1)

    stage_a_diag(0, 0)
    if n_pairs == 1:
        stage_bc_diag(0, 0)
        finalize(0)
        return
    close_and_open(0, 0, None)

    def step(g, pair):
        iq, t = pair
        opens_tile = t == 0
        ends_tile = t == iq
        for slot in range(2):
            in_slot = g % 2 == slot

            @pl.when(in_slot & ends_tile)
            def _():
                close_and_open(slot, iq, t)

            @pl.when(in_slot & jnp.logical_not(opens_tile | ends_tile))
            def _():
                stage_a(1 - slot, iq, t + 1)
                stage_bc(slot, iq, t)

        return jnp.where(ends_tile, iq + 1, iq), jnp.where(ends_tile, 0, t + 1)

    iq, t = lax.fori_loop(2, n_pairs - 1, step, (jnp.int32(1), jnp.int32(1)))
    stage_bc((n_pairs - 1) % 2, iq, t)
    finalize(iq)


def fox_attention(qkvg, caug, *, tk=1024):
    b, s, _ = qkvg.shape
    hh, d = FOX_HEADS, FOX_HEAD_DIM
    assert s % tk == 0
    head_cols = lambda group: pl.BlockSpec((1, s, d), lambda bi, h: (bi, 0, group * hh + h))
    return pl.pallas_call(
        functools.partial(_attn_kernel, tk=tk),
        grid=(b, hh),
        in_specs=[head_cols(0), head_cols(1), pl.BlockSpec((1, s, LANES), lambda bi, h: (bi, 0, 0)),
                  head_cols(2), head_cols(3)],
        out_specs=head_cols(0),
        out_shape=jax.ShapeDtypeStruct((b, s, hh * d), BF16),
        scratch_shapes=[
            pltpu.VMEM((s, d + LANES), BF16),
            pltpu.VMEM((d + STRIP, s), BF16),
            pltpu.VMEM((s, d + LANES), BF16),
            pltpu.VMEM((tk, tk), F32), pltpu.VMEM((tk, tk), F32),
            pltpu.VMEM((1, tk), F32), pltpu.VMEM((1, tk), F32),
            pltpu.VMEM((d + STRIP, tk), F32),
            pltpu.VMEM((1, tk), F32),
        ],
        compiler_params=_params("parallel", "arbitrary"),
        name="fox_attention",
    )(qkvg, qkvg, caug, qkvg, qkvg)


def _conv_kernel(a_ref, b_ref, g_ref, ah_ref, bh_ref, dw_ref, dwb_ref, lng_ref, lnb_ref, o_ref,
                 u_ref, y_ref, *, rows):
    ts, c = o_ref.shape[1], o_ref.shape[2]
    first = pl.program_id(1) == 0

    def glu(a, b):
        return a.astype(F32) * _sigmoid(b.astype(F32))

    n_chunks = c // LANES

    def put_u(row, value):
        for ci in range(n_chunks):
            u_ref[ci, pl.ds(row, value.shape[0]), :] = value[:, ci * LANES:(ci + 1) * LANES]

    put_u(0, jnp.where(first, 0.0, glu(ah_ref[0], bh_ref[0])))

    def glu_strip(i, carry):
        r = pl.multiple_of(i * STRIP, STRIP)
        put_u(CONV_HALO + r, glu(a_ref[0, pl.ds(r, STRIP), :], b_ref[0, pl.ds(r, STRIP), :]))
        return carry

    lax.fori_loop(0, ts // STRIP, glu_strip, 0)

    shift = CONV_HALO - (CONV_KERNEL - 1)

    def chunk(ci, carry):
        lanes = pl.ds(pl.multiple_of(ci * LANES, LANES), LANES)
        taps = dw_ref[:, lanes]
        for r0 in range(0, ts, rows):
            u = [u_ref[ci, pl.ds(r0 + shift + m, SUBLANES, stride=SUBLANES), :]
                 for m in range(rows // SUBLANES + CONV_KERNEL - 1)]
            for j in range(rows // SUBLANES):
                acc = taps[0:1, :] * u[j]
                for kk in range(1, CONV_KERNEL):
                    acc = acc + taps[kk:kk + 1, :] * u[j + kk]
                y_ref[ci, pl.ds(r0 + j, SUBLANES, stride=SUBLANES), :] = acc
        return carry

    lax.fori_loop(0, n_chunks, chunk, 0)

    def norm_strip(i, carry):
        r = pl.multiple_of(i * STRIP, STRIP)
        y = jnp.concatenate([y_ref[ci, pl.ds(r, STRIP), :] for ci in range(n_chunks)], axis=1) + dwb_ref[...]
        mu = jnp.mean(y, axis=-1, keepdims=True)
        yc = y - mu
        var = jnp.mean(yc * yc, axis=-1, keepdims=True)
        z = yc * lax.rsqrt(var + LN_EPS) * lng_ref[...] + lnb_ref[...]
        gate = g_ref[0, pl.ds(r, STRIP), :].astype(F32)
        o_ref[0, pl.ds(r, STRIP), :] = (_silu(z) * _silu(gate)).astype(o_ref.dtype)
        return carry

    lax.fori_loop(0, ts // STRIP, norm_strip, 0, unroll=4)


def conv_module(abg, dw, dw_b, ln_g, ln_b, *, ts=256):
    b, s, c3 = abg.shape
    c = c3 // 3
    per = ts // CONV_HALO
    rows = SUBLANES * SUBLANES
    assert s % ts == 0 and ts % CONV_HALO == 0 and ts % rows == 0 and c % LANES == 0
    tile = lambda col: pl.BlockSpec((1, ts, c), lambda bi, i: (bi, i, col))
    halo = lambda col: pl.BlockSpec((1, CONV_HALO, c), lambda bi, i: (bi, jnp.maximum(i * per - 1, 0), col))
    vec = pl.BlockSpec((1, c), lambda bi, i: (0, 0))
    return pl.pallas_call(
        functools.partial(_conv_kernel, rows=rows),
        grid=(b, s // ts),
        in_specs=[tile(0), tile(1), tile(2), halo(0), halo(1),
                  pl.BlockSpec((CONV_KERNEL + 1, c), lambda bi, i: (0, 0)), vec, vec, vec],
        out_specs=pl.BlockSpec((1, ts, c), lambda bi, i: (bi, i, 0)),
        out_shape=jax.ShapeDtypeStruct((b, s, c), BF16),
        scratch_shapes=[pltpu.VMEM((c // LANES, ts + CONV_HALO, LANES), F32),
                        pltpu.VMEM((c // LANES, ts, LANES), F32)],
        compiler_params=_params("parallel", "arbitrary"),
        name="conv_module",
    )(abg, abg, abg, abg, abg, dw, dw_b, ln_g, ln_b)


def _out_kernel(y_ref, w_ref, h_ref, g_ref, *out_refs, last):
    h_new = h_ref[...] + jnp.dot(y_ref[...], w_ref[...], preferred_element_type=F32)
    normed = h_new * lax.rsqrt(jnp.mean(h_new * h_new, axis=-1, keepdims=True) + RMS_EPS) * g_ref[...]
    if last:
        out_refs[0][...] = normed
    else:
        out_refs[0][...] = h_new
        out_refs[1][...] = normed.astype(BF16)


def out_projection(y, w, layer, h, g, *, last, tm=512):
    m, k = y.shape
    d = w.shape[1]
    assert m % tm == 0 and w.shape[0] % k == 0 and layer < w.shape[0] // k
    row = lambda width: pl.BlockSpec((tm, width), lambda i: (i, 0))
    if last:
        out_specs, out_shape = row(d), jax.ShapeDtypeStruct((m, d), F32)
    else:
        out_specs = [row(d), row(d)]
        out_shape = [jax.ShapeDtypeStruct((m, d), F32), jax.ShapeDtypeStruct((m, d), BF16)]
    return pl.pallas_call(
        functools.partial(_out_kernel, last=last),
        grid=(m // tm,),
        in_specs=[row(k), pl.BlockSpec((k, d), lambda i: (layer, 0)), row(d),
                  pl.BlockSpec((1, d), lambda i: (0, 0))],
        out_specs=out_specs,
        out_shape=out_shape,
        compiler_params=_params("parallel"),
        name="out_proj",
    )(y, w, h, g)


def _forget_weights(w_f, b_f):
    lane = jnp.arange(LANES)
    spread = ((lane[None, :] // HEAD_LANES == jnp.arange(FOX_HEADS)[:, None])
              & (lane[None, :] % HEAD_LANES < 3)).astype(F32)
    wf = jnp.dot(w_f, spread, precision=lax.Precision.HIGHEST).astype(BF16)
    bf = jnp.dot(b_f[None, :], spread, precision=lax.Precision.HIGHEST)
    return wf, bf


def kernel(x, norm_g, fox_w_in, fox_b_f, fox_w_out, conv_w_in, conv_b_in, conv_dw, conv_dw_b, conv_ln_g,
           conv_ln_b, conv_w_out, final_norm_g):
    b, s, d = x.shape
    m = b * s
    h = x.reshape(m, d)
    fox_w = fox_w_in[:, :, :4 * FOX_WIDTH].astype(BF16).reshape(-1, 4 * FOX_WIDTH)
    fox_wf = fox_w_in[:, :, 4 * FOX_WIDTH:]
    conv_w = conv_w_in.reshape(-1, conv_w_in.shape[-1])
    fox_w_o = fox_w_out.astype(BF16).reshape(-1, d)
    conv_w_o = conv_w_out.astype(BF16).reshape(-1, d)
    for i in range(DEPTH):
        j = i // 2
        if i % 2 == 0:
            wf, bf = _forget_weights(fox_wf[j], fox_b_f[j])
            q_scale = dict(scaled_cols=FOX_WIDTH, scale=FOX_HEAD_DIM ** -0.5 * LOG2E)
            no_bias = jnp.zeros((1, 4 * FOX_WIDTH), F32)
            if i == 0:
                qkvg, hn = projection(h, fox_w, j, 4 * FOX_WIDTH, no_bias, norm_gain=norm_g[0][None, :], **q_scale)
            else:
                qkvg = projection(hn, fox_w, j, 4 * FOX_WIDTH, no_bias, tn=2048, **q_scale)
            caug = forget_gate(hn.reshape(b, s, d), wf, bf)
            y = fox_attention(qkvg.reshape(b, s, -1), caug).reshape(m, FOX_WIDTH)
            w_out = fox_w_o
        else:
            abg = projection(hn, conv_w, j, 3 * CONV_CHANNELS, conv_b_in[j][None, :]).reshape(b, s, -1)
            dw = jnp.concatenate([conv_dw[j], jnp.zeros((1, CONV_CHANNELS), F32)], axis=0)
            y = conv_module(abg, dw, conv_dw_b[j][None, :], conv_ln_g[j][None, :],
                            conv_ln_b[j][None, :]).reshape(m, CONV_CHANNELS)
            w_out = conv_w_o
        if i + 1 < DEPTH:
            h, hn = out_projection(y, w_out, j, h, norm_g[i + 1][None, :], last=False)
        else:
            out = out_projection(y, w_out, j, h, final_norm_g[None, :], last=True)
    return out.reshape(b, s, d)
```
